```python
import jax, jax.numpy as jnp
from jax import lax
import numpy as np

D_MODEL = 1024
BATCH = 2
SEQ = 8192
DEPTH = 2

CHUNK = 64
N_META = 16
D_CONV = D_MODEL
CONV_W = 3
N_FOX_HEADS = 16
FOX_HEAD_DIM = 64
D_ATT = N_FOX_HEADS * FOX_HEAD_DIM
N_GROUPS = 4
EXPERTS_PER_GROUP = 8
N_EXPERTS = N_GROUPS * EXPERTS_PER_GROUP
TOP_K_IN_GROUP = 2
D_EXPERT = 512
Q_BLOCK = 128
MOE_BLOCK = 128
EPS = 1e-6
SPLIT_SIZES = (D_CONV, D_CONV, D_CONV, D_ATT, D_ATT, D_ATT, N_FOX_HEADS, D_MODEL, D_MODEL)
IN_COLS = sum(SPLIT_SIZES)

kernel_name = "hybrid_conv_fox_hiermoe_block"


def rms_norm(x, g):
    xf = x.astype(jnp.float32)
    y = xf * lax.rsqrt(jnp.mean(xf * xf, axis=-1, keepdims=True) + EPS)
    return (y * g.astype(jnp.float32)).astype(x.dtype)


def short_gated_conv(b_gate, c_gate, u, conv_w):
    z = c_gate * u
    L = z.shape[1]
    zp = jnp.pad(z, ((0, 0), (CONV_W - 1, 0), (0, 0)))
    y = zp[:, 0:L] * conv_w[0]
    for i in range(1, CONV_W):
        y = y + zp[:, i:i + L] * conv_w[i]
    return b_gate * y


def forgetting_attention(q, k, v, f_logit, q_g, k_g):
    Bn, L, H, hd = q.shape
    q = rms_norm(q, q_g)
    k = rms_norm(k, k_g)
    c = jnp.cumsum(jax.nn.log_sigmoid(f_logit.astype(jnp.float32)), axis=1)
    n_blk = -(-L // Q_BLOCK)
    Lp = n_blk * Q_BLOCK
    pad = Lp - L
    q = jnp.pad(q, ((0, 0), (0, pad), (0, 0), (0, 0)))
    k = jnp.pad(k, ((0, 0), (0, pad), (0, 0), (0, 0)))
    v = jnp.pad(v, ((0, 0), (0, pad), (0, 0), (0, 0)))
    c = jnp.pad(c, ((0, 0), (0, pad), (0, 0)), mode="edge")
    c_k = c.transpose(0, 2, 1)
    q_blocks = q.reshape(Bn, n_blk, Q_BLOCK, H, hd).transpose(1, 0, 2, 3, 4)
    c_blocks = c.reshape(Bn, n_blk, Q_BLOCK, H).transpose(1, 0, 3, 2)
    k_pos = jnp.arange(Lp)
    scale = FOX_HEAD_DIM ** -0.5

    def one_block(args):
        i, q_i, c_i = args
        s = jnp.einsum('bqhd,bkhd->bhqk', q_i, k).astype(jnp.float32) * scale
        s = s + (c_i[..., :, None] - c_k[..., None, :])
        q_pos = i * Q_BLOCK + jnp.arange(Q_BLOCK)
        s = jnp.where(k_pos[None, :] <= q_pos[:, None], s, -jnp.inf)
        p = jax.nn.softmax(s, axis=-1).astype(v.dtype)
        return jnp.einsum('bhqk,bkhd->bqhd', p, v)

    o = lax.map(one_block, (jnp.arange(n_blk), q_blocks, c_blocks))
    return o.transpose(1, 0, 2, 3, 4).reshape(Bn, Lp, H * hd)[:, :L]


def hierarchical_moe(x, w_rg, b_rg, w_re, b_re, w_gate, w_up, w_down):
    Bn, L, D = x.shape
    xf = x.reshape(-1, D)
    N = xf.shape[0]
    g_logits = (xf @ w_rg + b_rg).astype(jnp.float32)
    g_prob = jax.nn.softmax(g_logits, axis=-1)
    g_idx = jnp.argmax(g_logits, axis=-1)
    g_w = jnp.take_along_axis(g_prob, g_idx[:, None], axis=-1)
    e_logits = (xf @ w_re + b_re).astype(jnp.float32).reshape(N, N_GROUPS, EXPERTS_PER_GROUP)
    e_logits = jnp.take_along_axis(e_logits, g_idx[:, None, None], axis=1)[:, 0]
    top_p, top_j = lax.top_k(jax.nn.softmax(e_logits, axis=-1), TOP_K_IN_GROUP)
    weights = g_w * top_p / jnp.sum(top_p, axis=-1, keepdims=True)
    expert = g_idx[:, None] * EXPERTS_PER_GROUP + top_j

    A = N * TOP_K_IN_GROUP
    eid = expert.reshape(-1).astype(jnp.int32)
    tok = jnp.repeat(jnp.arange(N, dtype=jnp.int32), TOP_K_IN_GROUP)
    wt = weights.reshape(-1)
    order = jnp.argsort(eid)
    s_eid = eid[order]
    counts = jnp.zeros((N_EXPERTS,), jnp.int32).at[eid].add(1)
    pcounts = (counts + MOE_BLOCK - 1) // MOE_BLOCK * MOE_BLOCK
    start = jnp.cumsum(counts) - counts
    pend = jnp.cumsum(pcounts)
    pstart = pend - pcounts
    dest = pstart[s_eid] + jnp.arange(A, dtype=jnp.int32) - start[s_eid]
    n_blocks = -(-(A + N_EXPERTS * (MOE_BLOCK - 1)) // MOE_BLOCK)
    P = n_blocks * MOE_BLOCK
    slot_tok = jnp.full((P,), N, jnp.int32).at[dest].set(tok[order])
    slot_w = jnp.zeros((P,), wt.dtype).at[dest].set(wt[order])
    blk_eid = jnp.minimum(jnp.searchsorted(pend, jnp.arange(n_blocks, dtype=jnp.int32) * MOE_BLOCK, side='right'), N_EXPERTS - 1)
    x_pad = jnp.concatenate([xf, jnp.zeros((1, D), xf.dtype)], axis=0)
    xs = x_pad[slot_tok].reshape(n_blocks, MOE_BLOCK, D)

    def expert_block(args):
        e, xb = args
        h = jax.nn.silu(xb @ w_gate[e]) * (xb @ w_up[e])
        return h @ w_down[e]

    ys = lax.map(expert_block, (blk_eid, xs)).reshape(P, D)
    ys = ys * slot_w[:, None]
    out = jnp.zeros((N + 1, D), ys.dtype).at[slot_tok].add(ys)[:N]
    return out.reshape(Bn, L, D).astype(x.dtype)


def setup_inputs(seed: int = 0) -> dict:
    key = jax.random.key(seed)
    ks = jax.random.split(key, 24)
    nrm = lambda k, shape, s: jax.random.normal(k, shape, jnp.float32) * s
    Dm = D_MODEL
    return {
        "x": nrm(ks[0], (BATCH, SEQ, Dm), 1.0),
        "meta_tokens": nrm(ks[1], (N_META, Dm), 1.0),
        "norm1_g": 1.0 + nrm(ks[2], (DEPTH, Dm), 0.02),
        "w_in": nrm(ks[3], (DEPTH, Dm, IN_COLS), Dm ** -0.5),
        "b_forget": jax.random.uniform(ks[4], (DEPTH, N_FOX_HEADS), jnp.float32, 1.0, 4.0),
        "conv_w": nrm(ks[5], (DEPTH, CONV_W, D_CONV), CONV_W ** -0.5),
        "w_conv_out": nrm(ks[6], (DEPTH, D_CONV, Dm), D_CONV ** -0.5),
        "q_norm_g": 1.0 + nrm(ks[7], (DEPTH, FOX_HEAD_DIM), 0.02),
        "k_norm_g": 1.0 + nrm(ks[8], (DEPTH, FOX_HEAD_DIM), 0.02),
        "w_att_out": nrm(ks[9], (DEPTH, D_ATT, Dm), D_ATT ** -0.5),
        "w_merge_out": nrm(ks[10], (DEPTH, Dm, Dm), Dm ** -0.5),
        "norm2_g": 1.0 + nrm(ks[11], (DEPTH, Dm), 0.02),
        "w_router_group": nrm(ks[12], (DEPTH, Dm, N_GROUPS), Dm ** -0.5),
        "b_router_group": nrm(ks[13], (DEPTH, N_GROUPS), 0.01),
        "w_router_expert": nrm(ks[14], (DEPTH, Dm, N_EXPERTS), Dm ** -0.5),
        "b_router_expert": nrm(ks[15], (DEPTH, N_EXPERTS), 0.01),
        "w_exp_gate": nrm(ks[16], (DEPTH, N_EXPERTS, Dm, D_EXPERT), Dm ** -0.5),
        "w_exp_up": nrm(ks[17], (DEPTH, N_EXPERTS, Dm, D_EXPERT), Dm ** -0.5),
        "w_exp_down": nrm(ks[18], (DEPTH, N_EXPERTS, D_EXPERT, Dm), D_EXPERT ** -0.5),
    }


def reference(x, meta_tokens, norm1_g, w_in, b_forget, conv_w, w_conv_out, q_norm_g, k_norm_g,
              w_att_out, w_merge_out, norm2_g, w_router_group, b_router_group, w_router_expert,
              b_router_expert, w_exp_gate, w_exp_up, w_exp_down):
    Bn, S, Dm = x.shape
    meta = jnp.broadcast_to(meta_tokens[None].astype(x.dtype), (Bn, N_META, Dm))
    h = jnp.concatenate([meta, x], axis=1)
    L = h.shape[1]
    split_at = list(np.cumsum(SPLIT_SIZES)[:-1])
    for l in range(DEPTH):
        hn = rms_norm(h, norm1_g[l])
        proj = hn @ w_in[l]
        b_g, c_g, u, q, k, v, f_logit, ga, gb = jnp.split(proj, split_at, axis=-1)
        y_a = short_gated_conv(b_g, c_g, u, conv_w[l]) @ w_conv_out[l]
        q = q.reshape(Bn, L, N_FOX_HEADS, FOX_HEAD_DIM)
        k = k.reshape(Bn, L, N_FOX_HEADS, FOX_HEAD_DIM)
        v = v.reshape(Bn, L, N_FOX_HEADS, FOX_HEAD_DIM)
        attn = forgetting_attention(q, k, v, f_logit + b_forget[l], q_norm_g[l], k_norm_g[l])
        y_b = attn @ w_att_out[l]
        y = (jax.nn.sigmoid(ga) * y_a + jax.nn.sigmoid(gb) * y_b) @ w_merge_out[l]
        h = h + y
        h = h + hierarchical_moe(rms_norm(h, norm2_g[l]), w_router_group[l], b_router_group[l],
                                 w_router_expert[l], b_router_expert[l], w_exp_gate[l],
                                 w_exp_up[l], w_exp_down[l])
    return h[:, N_META:]
```

```python
import functools

import jax
import jax.numpy as jnp
from jax import lax
from jax.experimental import pallas as pl
from jax.experimental.pallas import tpu as pltpu

F32 = jnp.float32
BF16 = jnp.bfloat16
I32 = jnp.int32

N_META = 16
N_HEADS = 16
HEAD_DIM = 64
N_GROUPS = 4
EXPERTS_PER_GROUP = 8
N_EXPERTS = N_GROUPS * EXPERTS_PER_GROUP
EPS = 1e-6
NEG = -1e30

LANES = 128
SEQ_ALIGN = 256
ROUTE_LANES = 4 + N_EXPERTS

TM_INPROJ = 768
TR_PREP = 256
TQ_ATTN = 256
TK_ATTN = 256
TM_MERGE = 384
TP_POS = 256
BM_EXPERT = 256
TC_DISPATCH = 512
TC_COMBINE = 256

VMEM_LIMIT = 56 * 1024 * 1024


def _cparams(sem, vmem=VMEM_LIMIT):
    return pltpu.CompilerParams(dimension_semantics=sem, vmem_limit_bytes=vmem)


def _inproj_kernel(x_ref, g_ref, w_ref, wf_ref, o_ref, f_ref, hn_ref):
    @pl.when(pl.program_id(1) == 0)
    def _():
        x = x_ref[...]
        r = lax.rsqrt(jnp.mean(x * x, axis=-1, keepdims=True) + EPS)
        hn = (x * r * g_ref[...]).astype(BF16)
        hn_ref[...] = hn
        f_ref[...] = jnp.dot(hn, wf_ref[...], preferred_element_type=F32)

    o_ref[...] = jnp.dot(hn_ref[...], w_ref[...], preferred_element_type=F32).astype(BF16)


def _inproj(h, g, w_all, w_f):
    n, d = h.shape
    cols = w_all.shape[1]
    tm, tn = TM_INPROJ, 1024
    return pl.pallas_call(
        _inproj_kernel,
        grid=(n // tm, cols // tn),
        in_specs=[
            pl.BlockSpec((tm, d), lambda i, j: (i, 0)),
            pl.BlockSpec((1, d), lambda i, j: (0, 0)),
            pl.BlockSpec((d, tn), lambda i, j: (0, j)),
            pl.BlockSpec((d, LANES), lambda i, j: (0, 0)),
        ],
        out_specs=[
            pl.BlockSpec((tm, tn), lambda i, j: (i, j)),
            pl.BlockSpec((tm, LANES), lambda i, j: (i, 0)),
        ],
        out_shape=[
            jax.ShapeDtypeStruct((n, cols), BF16),
            jax.ShapeDtypeStruct((n, LANES), F32),
        ],
        scratch_shapes=[pltpu.VMEM((tm, d), BF16)],
        compiler_params=_cparams(("parallel", "arbitrary")),
        name="inproj",
    )(h, g, w_all, w_f)


def _split3(c):
    hi = c.astype(BF16).astype(F32)
    r1 = c - hi
    mid = r1.astype(BF16).astype(F32)
    lo = (r1 - mid).astype(BF16).astype(F32)
    return hi, mid, lo


def _prep_kernel(q_ref, k_ref, v_ref, f_ref, bf_ref, gq_ref, gk_ref,
                 qa_ref, ka_ref, va_ref, carry_ref):
    tr = f_ref.shape[0]

    @pl.when(pl.program_id(1) == 0)
    def _():
        carry_ref[...] = jnp.zeros_like(carry_ref)

    f = f_ref[...] + bf_ref[...]
    lf = jnp.minimum(f, 0.0) - jnp.log1p(jnp.exp(-jnp.abs(f)))
    row = lax.broadcasted_iota(I32, (tr, tr), 0)
    col = lax.broadcasted_iota(I32, (tr, tr), 1)
    tril = (col <= row).astype(F32)
    c = jnp.dot(tril, lf, precision=lax.Precision.HIGHEST,
                preferred_element_type=F32) + carry_ref[...]
    carry_ref[...] = c[tr - 1:tr, :]
    c_hi, c_mid, c_lo = _split3(c)

    lane = lax.broadcasted_iota(I32, (tr, LANES), 1)
    low = lane < HEAD_DIM
    ones_q = ((lane >= HEAD_DIM + 3) & (lane < HEAD_DIM + 6)).astype(F32)
    ones_k = ((lane >= HEAD_DIM) & (lane < HEAD_DIM + 3)).astype(F32)
    ones_v = (lane == HEAD_DIM).astype(F32)

    def norm_pair(x, g):
        sq = x * x
        ss_all = jnp.sum(sq, axis=1, keepdims=True)
        ss_lo = jnp.sum(jnp.where(low, sq, 0.0), axis=1, keepdims=True)
        ms = jnp.where(low, ss_lo, ss_all - ss_lo) * (1.0 / HEAD_DIM)
        return x * lax.rsqrt(ms + EPS) * g

    for p in range(N_HEADS // 2):
        sl = slice(p * LANES, (p + 1) * LANES)
        qn = norm_pair(q_ref[:, sl].astype(F32), gq_ref[...])
        kn = norm_pair(k_ref[:, sl].astype(F32), gk_ref[...])
        vv = v_ref[:, sl].astype(F32)
        for par in range(2):
            h = 2 * p + par
            if par == 0:
                qh, kh, vh = qn, kn, vv
            else:
                qh = pltpu.roll(qn, HEAD_DIM, 1)
                kh = pltpu.roll(kn, HEAD_DIM, 1)
                vh = pltpu.roll(vv, HEAD_DIM, 1)
            ch, cm, cl = c_hi[:, h:h + 1], c_mid[:, h:h + 1], c_lo[:, h:h + 1]
            ext_q = jnp.where(lane == HEAD_DIM, ch,
                              jnp.where(lane == HEAD_DIM + 1, cm,
                                        jnp.where(lane == HEAD_DIM + 2, cl, ones_q)))
            ext_k = jnp.where(lane == HEAD_DIM + 3, -ch,
                              jnp.where(lane == HEAD_DIM + 4, -cm,
                                        jnp.where(lane == HEAD_DIM + 5, -cl, ones_k)))
            qa_ref[0, h] = jnp.where(low, qh, ext_q).astype(BF16)
            ka_ref[0, h] = jnp.where(low, kh, ext_k).astype(BF16)
            va_ref[0, h] = jnp.where(low, vh, ones_v).astype(BF16)


def _prep(proj, f, bf, gq2, gk2, batch, lp):
    tr = TR_PREP
    nr = lp // tr
    aug = jax.ShapeDtypeStruct((batch, N_HEADS, lp, LANES), BF16)
    aug_spec = pl.BlockSpec((1, N_HEADS, tr, LANES), lambda b, r: (b, 0, r, 0))
    vec = pl.BlockSpec((1, LANES), lambda b, r: (0, 0))
    return pl.pallas_call(
        _prep_kernel,
        grid=(batch, nr),
        in_specs=[
            pl.BlockSpec((tr, 1024), lambda b, r: (b * nr + r, 3)),
            pl.BlockSpec((tr, 1024), lambda b, r: (b * nr + r, 4)),
            pl.BlockSpec((tr, 1024), lambda b, r: (b * nr + r, 5)),
            pl.BlockSpec((tr, LANES), lambda b, r: (b * nr + r, 0)),
            vec, vec, vec,
        ],
        out_specs=[aug_spec, aug_spec, aug_spec],
        out_shape=[aug, aug, aug],
        scratch_shapes=[pltpu.VMEM((1, LANES), F32)],
        compiler_params=_cparams(("parallel", "arbitrary")),
        name="prep",
    )(proj, proj, proj, f, bf, gq2, gk2)


def _attn_kernel(qa_ref, ka_ref, va_ref, o_ref, *, tq, tk):
    i = pl.program_id(2)
    n_diag = tq // tk
    n_full = i * n_diag
    qs = [qa_ref[0, hh] for hh in range(2)]

    def tile(hh, j, m, acc, masked):
        start = pl.multiple_of(j * tk, tk)
        k = ka_ref[0, hh, pl.ds(start, tk), :]
        v = va_ref[0, hh, pl.ds(start, tk), :]
        s = lax.dot_general(qs[hh], k, (((1,), (1,)), ((), ())),
                            preferred_element_type=F32)
        if masked:
            qpos = i * tq + lax.broadcasted_iota(I32, (tq, tk), 0)
            kpos = j * tk + lax.broadcasted_iota(I32, (tq, tk), 1)
            s = jnp.where(kpos <= qpos, s, NEG)
        m_new = jnp.maximum(m, jnp.max(s, axis=1, keepdims=True))
        alpha = jnp.exp(m - m_new)
        p = jnp.exp(s - m_new)
        acc = alpha * acc + jnp.dot(p.astype(BF16), v, preferred_element_type=F32)
        return m_new, acc

    def body(j, carry):
        m0, a0, m1, a1 = carry
        m0, a0 = tile(0, j, m0, a0, False)
        m1, a1 = tile(1, j, m1, a1, False)
        return m0, a0, m1, a1

    m_init = jnp.full((tq, 1), NEG, F32)
    a_init = jnp.zeros((tq, LANES), F32)
    m0, a0, m1, a1 = lax.fori_loop(0, n_full, body, (m_init, a_init, m_init, a_init))
    for d in range(n_diag):
        m0, a0 = tile(0, n_full + d, m0, a0, True)
        m1, a1 = tile(1, n_full + d, m1, a1, True)

    o0 = a0 / a0[:, HEAD_DIM:HEAD_DIM + 1]
    o1 = a1 / a1[:, HEAD_DIM:HEAD_DIM + 1]
    lane = lax.broadcasted_iota(I32, (tq, LANES), 1)
    o_ref[...] = jnp.where(lane < HEAD_DIM, o0, pltpu.roll(o1, HEAD_DIM, 1)).astype(BF16)


def _attention(qa, ka, va):
    batch, _, lp, _ = qa.shape
    tq, tk = TQ_ATTN, TK_ATTN
    nq = lp // tq
    return pl.pallas_call(
        functools.partial(_attn_kernel, tq=tq, tk=tk),
        grid=(batch, N_HEADS // 2, nq),
        in_specs=[
            pl.BlockSpec((1, 2, tq, LANES), lambda b, h, i: (b, h, i, 0)),
            pl.BlockSpec((1, 2, lp, LANES), lambda b, h, i: (b, h, 0, 0)),
            pl.BlockSpec((1, 2, lp, LANES), lambda b, h, i: (b, h, 0, 0)),
        ],
        out_specs=pl.BlockSpec((tq, LANES), lambda b, h, i: (b * nq + i, h)),
        out_shape=jax.ShapeDtypeStruct((batch * lp, N_HEADS * HEAD_DIM), BF16),
        compiler_params=_cparams(("parallel", "parallel", "arbitrary")),
        name="attn",
    )(qa, ka, va)


def _merge_kernel(b_ref, c_ref, u_ref, ch_ref, uh_ref, ga_ref, gb_ref, at_ref, h_ref,
                  cw_ref, wc_ref, wa_ref, wm_ref, g2_ref, wrh_ref, wrl_ref, br_ref,
                  ho_ref, xp_ref, rt_ref, *, tiles_per_seq):
    tm, d = h_ref.shape
    i = pl.program_id(0)

    z = c_ref[...].astype(F32) * u_ref[...].astype(F32)
    hv = (i % tiles_per_seq != 0).astype(F32)
    zh = ch_ref[...].astype(F32) * uh_ref[...].astype(F32) * hv
    hr = zh.shape[0]
    zm1 = zh[hr - 1:hr, :]
    zm2 = zh[hr - 2:hr - 1, :]
    rowid = lax.broadcasted_iota(I32, (tm, d), 0)
    zp1 = jnp.where(rowid == 0, zm1, pltpu.roll(z, 1, 0))
    zp2 = jnp.where(rowid == 0, zm2, jnp.where(rowid == 1, zm1, pltpu.roll(z, 2, 0)))
    cw = cw_ref[...]
    yc = b_ref[...].astype(F32) * (zp2 * cw[0:1, :] + zp1 * cw[1:2, :] + z * cw[2:3, :])

    y_a = jnp.dot(yc.astype(BF16), wc_ref[...], preferred_element_type=F32)
    y_b = jnp.dot(at_ref[...], wa_ref[...], preferred_element_type=F32)
    mix = (jax.nn.sigmoid(ga_ref[...].astype(F32)) * y_a
           + jax.nn.sigmoid(gb_ref[...].astype(F32)) * y_b)
    hnew = h_ref[...] + jnp.dot(mix.astype(BF16), wm_ref[...], preferred_element_type=F32)
    ho_ref[...] = hnew

    r = lax.rsqrt(jnp.mean(hnew * hnew, axis=-1, keepdims=True) + EPS)
    xn = hnew * r * g2_ref[...]
    x_hi = xn.astype(BF16)
    bits = lax.bitcast_convert_type(x_hi.astype(F32), I32)
    half = d // 2
    xp_ref[...] = (lax.shift_right_logical(bits[:, :half], 16)
                   | (bits[:, half:] & jnp.int32(-65536)))

    x_lo = (xn - x_hi.astype(F32)).astype(BF16)
    logits = (jnp.dot(x_hi, wrh_ref[...], preferred_element_type=F32)
              + jnp.dot(x_lo, wrh_ref[...], preferred_element_type=F32)
              + jnp.dot(x_hi, wrl_ref[...], preferred_element_type=F32)) + br_ref[...]

    lanei = lax.broadcasted_iota(I32, (tm, LANES), 1)
    lane = lanei.astype(F32)
    big = jnp.float32(1e9)
    is_g = lanei < N_GROUPS
    is_e = (lanei >= N_GROUPS) & (lanei < ROUTE_LANES)
    gl = jnp.where(is_g, logits, NEG)
    gmax = jnp.max(gl, axis=1, keepdims=True)
    gidx = jnp.min(jnp.where(gl == gmax, lane, big), axis=1, keepdims=True)
    gsum = jnp.sum(jnp.where(is_g, jnp.exp(gl - gmax), 0.0), axis=1, keepdims=True)
    grp = ((lanei - N_GROUPS) >> 3).astype(F32)
    el = jnp.where(is_e & (grp == gidx), logits, NEG)
    emax = jnp.max(el, axis=1, keepdims=True)
    i1 = jnp.min(jnp.where(el == emax, lane, big), axis=1, keepdims=True)
    el2 = jnp.where(lane == i1, NEG, el)
    e2max = jnp.max(el2, axis=1, keepdims=True)
    i2 = jnp.min(jnp.where(el2 == e2max, lane, big), axis=1, keepdims=True)
    t = jnp.exp(e2max - emax)
    w1 = 1.0 / (gsum * (1.0 + t))
    w2 = w1 * t
    rt_ref[...] = jnp.where(lanei == 0, i1 - N_GROUPS,
                            jnp.where(lanei == 1, i2 - N_GROUPS,
                                      jnp.where(lanei == 2, w1,
                                                jnp.where(lanei == 3, w2, 0.0))))


def _merge(proj, attn, h, cw, wc, wa, wm, g2, wr_hi, wr_lo, br, lp):
    n, d = h.shape
    tm = TM_MERGE
    halo = 16
    hb = tm // halo
    row = lambda c: pl.BlockSpec((tm, d), lambda i: (i, c))
    halo_spec = lambda c: pl.BlockSpec((halo, d), lambda i: (jnp.maximum(i * hb - 1, 0), c))
    full = lambda r, c: pl.BlockSpec((r, c), lambda i: (0, 0))
    return pl.pallas_call(
        functools.partial(_merge_kernel, tiles_per_seq=lp // tm),
        grid=(n // tm,),
        in_specs=[
            row(0), row(1), row(2), halo_spec(1), halo_spec(2), row(6), row(7),
            pl.BlockSpec((tm, d), lambda i: (i, 0)),
            pl.BlockSpec((tm, d), lambda i: (i, 0)),
            full(8, d), full(d, d), full(d, d), full(d, d), full(1, d),
            full(d, LANES), full(d, LANES), full(1, LANES),
        ],
        out_specs=[
            pl.BlockSpec((tm, d), lambda i: (i, 0)),
            pl.BlockSpec((tm, d // 2), lambda i: (i, 0)),
            pl.BlockSpec((tm, LANES), lambda i: (i, 0)),
        ],
        out_shape=[
            jax.ShapeDtypeStruct((n, d), F32),
            jax.ShapeDtypeStruct((n, d // 2), I32),
            jax.ShapeDtypeStruct((n, LANES), F32),
        ],
        compiler_params=_cparams(("parallel",)),
        name="merge",
    )(proj, proj, proj, proj, proj, proj, proj, attn, h,
      cw, wc, wa, wm, g2, wr_hi, wr_lo, br)


def _pos_kernel(rt_ref, pos_ref, blk_ref, cnt_ref, run_ref, pst_ref, *, bm):
    ph = pl.program_id(0)
    c = pl.program_id(1)
    tp = rt_ref.shape[0]
    nbp = blk_ref.shape[0]
    lane = lax.broadcasted_iota(I32, (tp, LANES), 1).astype(F32)
    rt = rt_ref[...]
    sel0 = lane == rt[:, 0:1]
    sel1 = lane == rt[:, 1:2]
    both = sel0.astype(F32) + sel1.astype(F32)
    colsum = jnp.sum(both, axis=0, keepdims=True)

    @pl.when((ph == 0) & (c == 0))
    def _():
        cnt_ref[...] = jnp.zeros_like(cnt_ref)

    @pl.when(ph == 0)
    def _():
        cnt_ref[...] += colsum

    @pl.when((ph == 1) & (c == 0))
    def _():
        nblk = jnp.floor((cnt_ref[...] + (bm - 1)) * (1.0 / bm))
        r0 = lax.broadcasted_iota(I32, (LANES, LANES), 0)
        c0 = lax.broadcasted_iota(I32, (LANES, LANES), 1)
        upper = (r0 < c0).astype(BF16)
        nblk8 = jnp.broadcast_to(nblk, (8, LANES)).astype(BF16)
        pstb = jnp.dot(nblk8, upper, preferred_element_type=F32)[0:1, :]
        pst_ref[...] = pstb * bm
        run_ref[...] = jnp.zeros_like(run_ref)
        pend = pstb + nblk
        bidx = lax.broadcasted_iota(I32, (nbp, LANES), 0).astype(F32)
        l2 = lax.broadcasted_iota(I32, (nbp, LANES), 1)
        started = ((pend <= bidx) & (l2 < N_EXPERTS)).astype(F32)
        eid = jnp.minimum(jnp.sum(started, axis=1, keepdims=True), N_EXPERTS - 1.0)
        nused = pend[:, N_EXPERTS - 1:N_EXPERTS]
        valid = (bidx < nused).astype(F32)
        blk_ref[...] = jnp.where(l2 == 0, eid, jnp.where(l2 == 1, valid, 0.0)).astype(I32)

    @pl.when(ph == 1)
    def _():
        r0 = lax.broadcasted_iota(I32, (tp, tp), 0)
        c0 = lax.broadcasted_iota(I32, (tp, tp), 1)
        lower = (c0 < r0).astype(BF16)
        before = jnp.dot(lower, both.astype(BF16), preferred_element_type=F32)
        base = before + run_ref[...] + pst_ref[...]
        d0 = jnp.sum(jnp.where(sel0, base, 0.0), axis=1, keepdims=True)
        d1 = jnp.sum(jnp.where(sel1, base, 0.0), axis=1, keepdims=True)
        pos_ref[...] = jnp.where(lane == 0.0, d0, jnp.where(lane == 1.0, d1, 0.0)).astype(I32)
        run_ref[...] += colsum


def _positions(route, nbp):
    n = route.shape[0]
    tp = TP_POS
    return pl.pallas_call(
        functools.partial(_pos_kernel, bm=BM_EXPERT),
        grid=(2, n // tp),
        in_specs=[pl.BlockSpec((tp, LANES), lambda ph, c: (c, 0))],
        out_specs=[
            pl.BlockSpec((tp, LANES), lambda ph, c: (c * ph, 0)),
            pl.BlockSpec((nbp, LANES), lambda ph, c: (0, 0)),
        ],
        out_shape=[
            jax.ShapeDtypeStruct((n, LANES), I32),
            jax.ShapeDtypeStruct((nbp, LANES), I32),
        ],
        scratch_shapes=[pltpu.VMEM((1, LANES), F32)] * 3,
        compiler_params=_cparams(("arbitrary", "arbitrary")),
        name="positions",
    )(route)


def _dispatch_kernel(d0_ref, d1_ref, xp_hbm, xs_in_hbm, xs_hbm, sem, *, tc):
    del xs_in_hbm
    base = pl.program_id(0) * tc

    def copies(t):
        src = xp_hbm.at[pl.ds(base + t, 1)]
        return (pltpu.make_async_copy(src, xs_hbm.at[pl.ds(d0_ref[t], 1)], sem),
                pltpu.make_async_copy(src, xs_hbm.at[pl.ds(d1_ref[t], 1)], sem))

    def issue(t, carry):
        a, b = copies(t)
        a.start()
        b.start()
        return carry

    def drain(t, carry):
        a, b = copies(t)
        a.wait()
        b.wait()
        return carry

    lax.fori_loop(0, tc, issue, 0)
    lax.fori_loop(0, tc, drain, 0)


def _dispatch(d0, d1, xp, xs_init):
    n = xp.shape[0]
    tc = TC_DISPATCH
    smem = pl.BlockSpec((tc,), lambda i: (i,), memory_space=pltpu.SMEM)
    return pl.pallas_call(
        functools.partial(_dispatch_kernel, tc=tc),
        grid=(n // tc,),
        in_specs=[smem, smem,
                  pl.BlockSpec(memory_space=pl.ANY), pl.BlockSpec(memory_space=pl.ANY)],
        out_specs=pl.BlockSpec(memory_space=pl.ANY),
        out_shape=jax.ShapeDtypeStruct(xs_init.shape, xs_init.dtype),
        scratch_shapes=[pltpu.SemaphoreType.DMA],
        input_output_aliases={3: 0},
        compiler_params=_cparams(("arbitrary",)),
        name="dispatch",
    )(d0, d1, xp, xs_init)


def _expert_kernel(eid_ref, valid_ref, xs_ref, wg_ref, wu_ref, wd_ref, ys_ref):
    b = pl.program_id(0)

    @pl.when(valid_ref[b] != 0)
    def _():
        w = xs_ref[...]
        half = w.shape[1]
        lo = lax.bitcast_convert_type(w << 16, F32).astype(BF16)
        hi = lax.bitcast_convert_type(w & jnp.int32(-65536), F32).astype(BF16)

        def proj(w_ref):
            return (jnp.dot(lo, w_ref[0, :half, :], preferred_element_type=F32)
                    + jnp.dot(hi, w_ref[0, half:, :], preferred_element_type=F32))

        hg = proj(wg_ref)
        hu = proj(wu_ref)
        act = (hg * jax.nn.sigmoid(hg) * hu).astype(BF16)
        ys_ref[...] = jnp.dot(act, wd_ref[0], preferred_element_type=F32)

    @pl.when(valid_ref[b] == 0)
    def _():
        ys_ref[...] = jnp.zeros_like(ys_ref)


def _experts(eid, valid, xs, wg, wu, wd):
    p, half = xs.shape
    d = 2 * half
    de = wg.shape[2]
    bm = BM_EXPERT
    return pl.pallas_call(
        _expert_kernel,
        grid_spec=pltpu.PrefetchScalarGridSpec(
            num_scalar_prefetch=2,
            grid=(p // bm,),
            in_specs=[
                pl.BlockSpec((bm, half), lambda b, e, v: (b, 0)),
                pl.BlockSpec((1, d, de), lambda b, e, v: (e[b], 0, 0)),
                pl.BlockSpec((1, d, de), lambda b, e, v: (e[b], 0, 0)),
                pl.BlockSpec((1, de, d), lambda b, e, v: (e[b], 0, 0)),
            ],
            out_specs=pl.BlockSpec((bm, d), lambda b, e, v: (b, 0)),
        ),
        out_shape=jax.ShapeDtypeStruct((p, d), F32),
        compiler_params=_cparams(("arbitrary",)),
        name="experts",
    )(eid, valid, xs, wg, wu, wd)


def _combine_kernel(d0_ref, d1_ref, h_ref, rt_ref, ys_hbm, o_ref, g0_ref, g1_ref, sem, *, tc):
    def copies(t):
        return (pltpu.make_async_copy(ys_hbm.at[pl.ds(d0_ref[t], 1)], g0_ref.at[pl.ds(t, 1)], sem),
                pltpu.make_async_copy(ys_hbm.at[pl.ds(d1_ref[t], 1)], g1_ref.at[pl.ds(t, 1)], sem))

    def issue(t, carry):
        a, b = copies(t)
        a.start()
        b.start()
        return carry

    def drain(t, carry):
        a, b = copies(t)
        a.wait()
        b.wait()
        return carry

    lax.fori_loop(0, tc, issue, 0)
    lax.fori_loop(0, tc, drain, 0)
    rt = rt_ref[...]
    o_ref[...] = h_ref[...] + rt[:, 2:3] * g0_ref[...] + rt[:, 3:4] * g1_ref[...]


def _combine(d0, d1, h, route, ys):
    n, d = h.shape
    tc = TC_COMBINE
    smem = pl.BlockSpec((tc,), lambda i: (i,), memory_space=pltpu.SMEM)
    return pl.pallas_call(
        functools.partial(_combine_kernel, tc=tc),
        grid=(n // tc,),
        in_specs=[smem, smem,
                  pl.BlockSpec((tc, d), lambda i: (i, 0)),
                  pl.BlockSpec((tc, LANES), lambda i: (i, 0)),
                  pl.BlockSpec(memory_space=pl.ANY)],
        out_specs=pl.BlockSpec((tc, d), lambda i: (i, 0)),
        out_shape=jax.ShapeDtypeStruct((n, d), F32),
        scratch_shapes=[pltpu.VMEM((tc, d), F32), pltpu.VMEM((tc, d), F32),
                        pltpu.SemaphoreType.DMA],
        compiler_params=_cparams(("arbitrary",)),
        name="combine",
    )(d0, d1, h, route, ys)


def _pad_lanes(v, width=LANES):
    return jnp.pad(v, [(0, 0)] * (v.ndim - 1) + [(0, width - v.shape[-1])])


def kernel(x, meta_tokens, norm1_g, w_in, b_forget, conv_w, w_conv_out, q_norm_g, k_norm_g,
           w_att_out, w_merge_out, norm2_g, w_router_group, b_router_group, w_router_expert,
           b_router_expert, w_exp_gate, w_exp_up, w_exp_down):
    batch, seq, d = x.shape
    depth = w_in.shape[0]
    length = seq + N_META
    lp = -(-length // SEQ_ALIGN) * SEQ_ALIGN
    n = batch * lp
    assert d == N_HEADS * HEAD_DIM == 1024
    assert lp % TQ_ATTN == 0 and lp % TM_MERGE == 0 and lp % TR_PREP == 0
    assert n % TM_INPROJ == 0 and n % TC_DISPATCH == 0 and n % TC_COMBINE == 0 and n % TP_POS == 0

    n_blocks = -(-(2 * n + N_EXPERTS * (BM_EXPERT - 1)) // BM_EXPERT)
    nbp = -(-n_blocks // 8) * 8
    p_rows = n_blocks * BM_EXPERT

    meta = jnp.broadcast_to(meta_tokens[None].astype(x.dtype), (batch, N_META, d))
    h = jnp.concatenate([meta, x, jnp.zeros((batch, lp - length, d), x.dtype)], axis=1)
    h = h.reshape(n, d)

    scale = HEAD_DIM ** -0.5
    for l in range(depth):
        w = w_in[l]
        w_all = jnp.concatenate([w[:, :6 * d], w[:, 6 * d + N_HEADS:]], axis=1).astype(BF16)
        w_f = _pad_lanes(w[:, 6 * d:6 * d + N_HEADS]).astype(BF16)
        proj, f = _inproj(h, norm1_g[l][None], w_all, w_f)

        bf = _pad_lanes(b_forget[l][None])
        gq2 = jnp.tile(q_norm_g[l], 2)[None] * scale
        gk2 = jnp.tile(k_norm_g[l], 2)[None]
        qa, ka, va = _prep(proj, f, bf, gq2, gk2, batch, lp)
        attn = _attention(qa, ka, va)

        w_r = _pad_lanes(jnp.concatenate([w_router_group[l], w_router_expert[l]], axis=1))
        wr_hi = w_r.astype(BF16)
        wr_lo = (w_r - wr_hi.astype(F32)).astype(BF16)
        b_r = _pad_lanes(jnp.concatenate([b_router_group[l], b_router_expert[l]])[None])
        cw = jnp.pad(conv_w[l], ((0, 8 - conv_w.shape[1]), (0, 0)))
        h_mid, xp, route = _merge(
            proj, attn, h, cw, w_conv_out[l].astype(BF16), w_att_out[l].astype(BF16),
            w_merge_out[l].astype(BF16), norm2_g[l][None], wr_hi, wr_lo, b_r, lp)

        pos, blk = _positions(route, nbp)
        d0, d1 = pos[:, 0], pos[:, 1]
        xs = _dispatch(d0, d1, xp, jnp.zeros((p_rows, d // 2), I32))
        ys = _experts(blk[:n_blocks, 0], blk[:n_blocks, 1], xs,
                      w_exp_gate[l].astype(BF16), w_exp_up[l].astype(BF16),
                      w_exp_down[l].astype(BF16))
        h = _combine(d0, d1, h_mid, route, ys)

    return h.reshape(batch, lp, d)[:, N_META:length]
```

```python
import functools

import jax
import jax.numpy as jnp
from jax import lax
from jax.experimental import pallas as pl
from jax.experimental.pallas import tpu as pltpu

F32 = jnp.float32
BF16 = jnp.bfloat16
I32 = jnp.int32

N_META = 16
N_HEADS = 16
HEAD_DIM = 64
N_GROUPS = 4
EXPERTS_PER_GROUP = 8
N_EXPERTS = N_GROUPS * EXPERTS_PER_GROUP
EPS = 1e-6
NEG = -1e30
LOG2E = 1.4426950408889634
BOUND_LANE = HEAD_DIM + 6
SAFE_BOUND_LOG2 = 50.0

LANES = 128
SEQ_ALIGN = 256
ROUTE_LANES = 4 + N_EXPERTS

TM_INPROJ = 768
TR_PREP = 256
TQ_ATTN = 256
TK_ATTN = 256
TM_MERGE = 384
TP_POS = 256
BM_EXPERT = 256
TC_DISPATCH = 512
TC_COMBINE = 256

VMEM_LIMIT = 56 * 1024 * 1024


def _cparams(sem, vmem=VMEM_LIMIT):
    return pltpu.CompilerParams(dimension_semantics=sem, vmem_limit_bytes=vmem)


def _inproj_kernel(x_ref, g_ref, w_ref, wf_ref, o_ref, f_ref, hn_ref):
    @pl.when(pl.program_id(1) == 0)
    def _():
        x = x_ref[...]
        r = lax.rsqrt(jnp.mean(x * x, axis=-1, keepdims=True) + EPS)
        hn = (x * r * g_ref[...]).astype(BF16)
        hn_ref[...] = hn
        f_ref[...] = jnp.dot(hn, wf_ref[...], preferred_element_type=F32)

    o_ref[...] = jnp.dot(hn_ref[...], w_ref[...], preferred_element_type=F32).astype(BF16)


def _inproj(h, g, w_all, w_f):
    n, d = h.shape
    cols = w_all.shape[1]
    tm, tn = TM_INPROJ, 1024
    return pl.pallas_call(
        _inproj_kernel,
        grid=(n // tm, cols // tn),
        in_specs=[
            pl.BlockSpec((tm, d), lambda i, j: (i, 0)),
            pl.BlockSpec((1, d), lambda i, j: (0, 0)),
            pl.BlockSpec((d, tn), lambda i, j: (0, j)),
            pl.BlockSpec((d, LANES), lambda i, j: (0, 0)),
        ],
        out_specs=[
            pl.BlockSpec((tm, tn), lambda i, j: (i, j)),
            pl.BlockSpec((tm, LANES), lambda i, j: (i, 0)),
        ],
        out_shape=[
            jax.ShapeDtypeStruct((n, cols), BF16),
            jax.ShapeDtypeStruct((n, LANES), F32),
        ],
        scratch_shapes=[pltpu.VMEM((tm, d), BF16)],
        compiler_params=_cparams(("parallel", "arbitrary")),
        name="inproj",
    )(h, g, w_all, w_f)


def _split3(c):
    hi = c.astype(BF16).astype(F32)
    r1 = c - hi
    mid = r1.astype(BF16).astype(F32)
    lo = (r1 - mid).astype(BF16).astype(F32)
    return hi, mid, lo


def _prep_kernel(q_ref, k_ref, v_ref, f_ref, bf_ref, gq_ref, gk_ref,
                 qa_ref, ka_ref, va_ref, carry_ref):
    tr = f_ref.shape[0]

    @pl.when(pl.program_id(1) == 0)
    def _():
        carry_ref[...] = jnp.zeros_like(carry_ref)

    f = f_ref[...] + bf_ref[...]
    lf = jnp.minimum(f, 0.0) - jnp.log1p(jnp.exp(-jnp.abs(f)))
    row = lax.broadcasted_iota(I32, (tr, tr), 0)
    col = lax.broadcasted_iota(I32, (tr, tr), 1)
    tril = (col <= row).astype(F32)
    c = jnp.dot(tril, lf, precision=lax.Precision.HIGHEST,
                preferred_element_type=F32) + carry_ref[...]
    carry_ref[...] = c[tr - 1:tr, :]
    c_hi, c_mid, c_lo = _split3(c * LOG2E)

    lane = lax.broadcasted_iota(I32, (tr, LANES), 1)
    low = lane < HEAD_DIM
    ones_q = ((lane >= HEAD_DIM + 3) & (lane < HEAD_DIM + 6)).astype(F32)
    ones_k = (((lane >= HEAD_DIM) & (lane < HEAD_DIM + 3)) | (lane == BOUND_LANE)).astype(F32)
    ones_v = (lane == HEAD_DIM).astype(F32)

    def norm_pair(x, g):
        sq = x * x
        ss_all = jnp.sum(sq, axis=1, keepdims=True)
        ss_lo = jnp.sum(jnp.where(low, sq, 0.0), axis=1, keepdims=True)
        ms = jnp.where(low, ss_lo, ss_all - ss_lo) * (1.0 / HEAD_DIM)
        return x * lax.rsqrt(ms + EPS) * g

    for p in range(N_HEADS // 2):
        sl = slice(p * LANES, (p + 1) * LANES)
        qn = norm_pair(q_ref[:, sl].astype(F32), gq_ref[...])
        kn = norm_pair(k_ref[:, sl].astype(F32), gk_ref[...])
        vv = v_ref[:, sl].astype(F32)
        for par in range(2):
            h = 2 * p + par
            if par == 0:
                qh, kh, vh = qn, kn, vv
            else:
                qh = pltpu.roll(qn, HEAD_DIM, 1)
                kh = pltpu.roll(kn, HEAD_DIM, 1)
                vh = pltpu.roll(vv, HEAD_DIM, 1)
            ch, cm, cl = c_hi[:, h:h + 1], c_mid[:, h:h + 1], c_lo[:, h:h + 1]
            ext_q = jnp.where(lane == HEAD_DIM, ch,
                              jnp.where(lane == HEAD_DIM + 1, cm,
                                        jnp.where(lane == HEAD_DIM + 2, cl, ones_q)))
            ext_k = jnp.where(lane == HEAD_DIM + 3, -ch,
                              jnp.where(lane == HEAD_DIM + 4, -cm,
                                        jnp.where(lane == HEAD_DIM + 5, -cl, ones_k)))
            qa_ref[0, h] = jnp.where(low, qh, ext_q).astype(BF16)
            ka_ref[0, h] = jnp.where(low, kh, ext_k).astype(BF16)
            va_ref[0, h] = jnp.where(low, vh, ones_v).astype(BF16)


def _prep(proj, f, bf, gq2, gk2, batch, lp):
    tr = TR_PREP
    nr = lp // tr
    aug = jax.ShapeDtypeStruct((batch, N_HEADS, lp, LANES), BF16)
    aug_spec = pl.BlockSpec((1, N_HEADS, tr, LANES), lambda b, r: (b, 0, r, 0))
    vec = pl.BlockSpec((1, LANES), lambda b, r: (0, 0))
    return pl.pallas_call(
        _prep_kernel,
        grid=(batch, nr),
        in_specs=[
            pl.BlockSpec((tr, 1024), lambda b, r: (b * nr + r, 3)),
            pl.BlockSpec((tr, 1024), lambda b, r: (b * nr + r, 4)),
            pl.BlockSpec((tr, 1024), lambda b, r: (b * nr + r, 5)),
            pl.BlockSpec((tr, LANES), lambda b, r: (b * nr + r, 0)),
            vec, vec, vec,
        ],
        out_specs=[aug_spec, aug_spec, aug_spec],
        out_shape=[aug, aug, aug],
        scratch_shapes=[pltpu.VMEM((1, LANES), F32)],
        compiler_params=_cparams(("parallel", "arbitrary")),
        name="prep",
    )(proj, proj, proj, f, bf, gq2, gk2)


def _attn_kernel(qa_ref, ka_ref, va_ref, o_ref, acc_ref, kmax_ref, s_ref, *, tq, tk):
    i = pl.program_id(2)
    lp = ka_ref.shape[2]
    assert tq == tk
    n_full = i
    n_diag = 1
    lane = lax.broadcasted_iota(I32, (tq, LANES), 1)
    low = lane < HEAD_DIM
    nt = (((1,), (1,)), ((), ()))

    @pl.when(i == 0)
    def _():
        lk = lax.broadcasted_iota(I32, (tk, LANES), 1) < HEAD_DIM
        for hh in range(2):
            def chunk(r, best, hh=hh):
                kk = ka_ref[0, hh, pl.ds(pl.multiple_of(r * tk, tk), tk), :].astype(F32)
                return jnp.maximum(best, jnp.sum(jnp.where(lk, kk * kk, 0.0), axis=1, keepdims=True))
            best = lax.fori_loop(0, lp // tk, chunk, jnp.zeros((tk, 1), F32))
            kmax_ref[hh] = jnp.broadcast_to(jnp.max(best, axis=0, keepdims=True), (8, LANES))

    def masked_logits(s, j):
        qpos = i * tq + lax.broadcasted_iota(I32, (tq, tk), 0)
        kpos = j * tk + lax.broadcasted_iota(I32, (tq, tk), 1)
        return jnp.where(kpos <= qpos, s, NEG)

    def kv(hh, j):
        start = pl.multiple_of(j * tk, tk)
        return (ka_ref[0, hh, pl.ds(start, tk), :], va_ref[0, hh, pl.ds(start, tk), :])

    qb = []
    worst = jnp.zeros((1, 1), F32)
    for hh in range(2):
        qf = qa_ref[0, hh].astype(F32)
        qn2 = jnp.sum(jnp.where(low, qf * qf, 0.0), axis=1, keepdims=True)
        bound = jnp.sqrt(qn2 * kmax_ref[hh][0:1, 0:1]) + 0.25
        qb.append(jnp.where(lane == BOUND_LANE, -bound, qf).astype(BF16))
        worst = jnp.maximum(worst, jnp.max(bound, axis=0, keepdims=True))
    safe = worst[0, 0] <= SAFE_BOUND_LOG2

    @pl.when(safe)
    def _():
        acc_ref[...] = jnp.zeros_like(acc_ref)

        def logits(j, buf, masked):
            for hh in range(2):
                k, _ = kv(hh, j)
                s = lax.dot_general(qb[hh], k, nt, preferred_element_type=F32)
                s_ref[buf, hh] = masked_logits(s, j) if masked else s

        def consume(j, buf):
            for hh in range(2):
                _, v = kv(hh, j)
                p = jnp.exp2(s_ref[buf, hh]).astype(BF16)
                acc_ref[hh] += jnp.dot(p, v, preferred_element_type=F32)

        def body(r, carry):
            j = 2 * r
            logits(j + 1, 1, False)
            consume(j, 0)
            logits(j + 2, 0, False)
            consume(j + 1, 1)
            return carry

        logits(0, 0, True)
        pairs = jnp.maximum(n_full - 1, 0) // 2
        lax.fori_loop(0, pairs, body, 0)
        jt = 2 * pairs
        consume(jt, 0)
        for extra in (1, 2):
            @pl.when(jt + extra <= n_full)
            def _(extra=extra):
                logits(jt + extra, 0, True)
                consume(jt + extra, 0)

    @pl.when(jnp.logical_not(safe))
    def _():
        qs = [qa_ref[0, hh] for hh in range(2)]

        def tile(hh, j, m, acc, masked):
            k, v = kv(hh, j)
            s = lax.dot_general(qs[hh], k, nt, preferred_element_type=F32)
            if masked:
                s = masked_logits(s, j)
            m_new = jnp.maximum(m, jnp.max(s, axis=1, keepdims=True))
            acc = jnp.exp2(m - m_new) * acc + jnp.dot(
                jnp.exp2(s - m_new).astype(BF16), v, preferred_element_type=F32)
            return m_new, acc

        def body(j, carry):
            m0, a0, m1, a1 = carry
            m0, a0 = tile(0, j, m0, a0, False)
            m1, a1 = tile(1, j, m1, a1, False)
            return m0, a0, m1, a1

        m_init = jnp.full((tq, 1), NEG, F32)
        a_init = jnp.zeros((tq, LANES), F32)
        m0, a0, m1, a1 = lax.fori_loop(0, n_full, body, (m_init, a_init, m_init, a_init))
        for d in range(n_diag):
            m0, a0 = tile(0, n_full + d, m0, a0, True)
            m1, a1 = tile(1, n_full + d, m1, a1, True)
        acc_ref[0] = a0
        acc_ref[1] = a1

    a0 = acc_ref[0]
    a1 = acc_ref[1]
    o0 = a0 / a0[:, HEAD_DIM:HEAD_DIM + 1]
    o1 = a1 / a1[:, HEAD_DIM:HEAD_DIM + 1]
    o_ref[...] = jnp.where(low, o0, pltpu.roll(o1, HEAD_DIM, 1)).astype(BF16)


def _attention(qa, ka, va):
    batch, _, lp, _ = qa.shape
    tq, tk = TQ_ATTN, TK_ATTN
    nq = lp // tq
    return pl.pallas_call(
        functools.partial(_attn_kernel, tq=tq, tk=tk),
        grid=(batch, N_HEADS // 2, nq),
        in_specs=[
            pl.BlockSpec((1, 2, tq, LANES), lambda b, h, i: (b, h, i, 0)),
            pl.BlockSpec((1, 2, lp, LANES), lambda b, h, i: (b, h, 0, 0)),
            pl.BlockSpec((1, 2, lp, LANES), lambda b, h, i: (b, h, 0, 0)),
        ],
        out_specs=pl.BlockSpec((tq, LANES), lambda b, h, i: (b * nq + i, h)),
        out_shape=jax.ShapeDtypeStruct((batch * lp, N_HEADS * HEAD_DIM), BF16),
        scratch_shapes=[pltpu.VMEM((2, tq, LANES), F32), pltpu.VMEM((2, 8, LANES), F32),
                        pltpu.VMEM((2, 2, tq, tk), F32)],
        compiler_params=_cparams(("parallel", "parallel", "arbitrary")),
        name="attn",
    )(qa, ka, va)


def _merge_kernel(b_ref, c_ref, u_ref, ch_ref, uh_ref, ga_ref, gb_ref, at_ref, h_ref,
                  cw_ref, wc_ref, wa_ref, wm_ref, g2_ref, wrh_ref, wrl_ref, br_ref,
                  ho_ref, xp_ref, rt_ref, *, tiles_per_seq):
    tm, d = h_ref.shape
    i = pl.program_id(0)

    z = c_ref[...].astype(F32) * u_ref[...].astype(F32)
    hv = (i % tiles_per_seq != 0).astype(F32)
    zh = ch_ref[...].astype(F32) * uh_ref[...].astype(F32) * hv
    hr = zh.shape[0]
    zm1 = zh[hr - 1:hr, :]
    zm2 = zh[hr - 2:hr - 1, :]
    rowid = lax.broadcasted_iota(I32, (tm, d), 0)
    zp1 = jnp.where(rowid == 0, zm1, pltpu.roll(z, 1, 0))
    zp2 = jnp.where(rowid == 0, zm2, jnp.where(rowid == 1, zm1, pltpu.roll(z, 2, 0)))
    cw = cw_ref[...]
    yc = b_ref[...].astype(F32) * (zp2 * cw[0:1, :] + zp1 * cw[1:2, :] + z * cw[2:3, :])

    y_a = jnp.dot(yc.astype(BF16), wc_ref[...], preferred_element_type=F32)
    y_b = jnp.dot(at_ref[...], wa_ref[...], preferred_element_type=F32)
    mix = (jax.nn.sigmoid(ga_ref[...].astype(F32)) * y_a
           + jax.nn.sigmoid(gb_ref[...].astype(F32)) * y_b)
    hnew = h_ref[...] + jnp.dot(mix.astype(BF16), wm_ref[...], preferred_element_type=F32)
    ho_ref[...] = hnew

    r = lax.rsqrt(jnp.mean(hnew * hnew, axis=-1, keepdims=True) + EPS)
    xn = hnew * r * g2_ref[...]
    x_hi = xn.astype(BF16)
    bits = lax.bitcast_convert_type(x_hi.astype(F32), I32)
    half = d // 2
    xp_ref[...] = (lax.shift_right_logical(bits[:, :half], 16)
                   | (bits[:, half:] & jnp.int32(-65536)))

    x_lo = (xn - x_hi.astype(F32)).astype(BF16)
    logits = (jnp.dot(x_hi, wrh_ref[...], preferred_element_type=F32)
              + jnp.dot(x_lo, wrh_ref[...], preferred_element_type=F32)
              + jnp.dot(x_hi, wrl_ref[...], preferred_element_type=F32)) + br_ref[...]

    lanei = lax.broadcasted_iota(I32, (tm, LANES), 1)
    lane = lanei.astype(F32)
    big = jnp.float32(1e9)
    is_g = lanei < N_GROUPS
    is_e = (lanei >= N_GROUPS) & (lanei < ROUTE_LANES)
    gl = jnp.where(is_g, logits, NEG)
    gmax = jnp.max(gl, axis=1, keepdims=True)
    gidx = jnp.min(jnp.where(gl == gmax, lane, big), axis=1, keepdims=True)
    gsum = jnp.sum(jnp.where(is_g, jnp.exp(gl - gmax), 0.0), axis=1, keepdims=True)
    grp = ((lanei - N_GROUPS) >> 3).astype(F32)
    el = jnp.where(is_e & (grp == gidx), logits, NEG)
    emax = jnp.max(el, axis=1, keepdims=True)
    i1 = jnp.min(jnp.where(el == emax, lane, big), axis=1, keepdims=True)
    el2 = jnp.where(lane == i1, NEG, el)
    e2max = jnp.max(el2, axis=1, keepdims=True)
    i2 = jnp.min(jnp.where(el2 == e2max, lane, big), axis=1, keepdims=True)
    t = jnp.exp(e2max - emax)
    w1 = 1.0 / (gsum * (1.0 + t))
    w2 = w1 * t
    rt_ref[...] = jnp.where(lanei == 0, i1 - N_GROUPS,
                            jnp.where(lanei == 1, i2 - N_GROUPS,
                                      jnp.where(lanei == 2, w1,
                                                jnp.where(lanei == 3, w2, 0.0))))


def _merge(proj, attn, h, cw, wc, wa, wm, g2, wr_hi, wr_lo, br, lp):
    n, d = h.shape
    tm = TM_MERGE
    halo = 16
    hb = tm // halo
    row = lambda c: pl.BlockSpec((tm, d), lambda i: (i, c))
    halo_spec = lambda c: pl.BlockSpec((halo, d), lambda i: (jnp.maximum(i * hb - 1, 0), c))
    full = lambda r, c: pl.BlockSpec((r, c), lambda i: (0, 0))
    return pl.pallas_call(
        functools.partial(_merge_kernel, tiles_per_seq=lp // tm),
        grid=(n // tm,),
        in_specs=[
            row(0), row(1), row(2), halo_spec(1), halo_spec(2), row(6), row(7),
            pl.BlockSpec((tm, d), lambda i: (i, 0)),
            pl.BlockSpec((tm, d), lambda i: (i, 0)),
            full(8, d), full(d, d), full(d, d), full(d, d), full(1, d),
            full(d, LANES), full(d, LANES), full(1, LANES),
        ],
        out_specs=[
            pl.BlockSpec((tm, d), lambda i: (i, 0)),
            pl.BlockSpec((tm, d // 2), lambda i: (i, 0)),
            pl.BlockSpec((tm, LANES), lambda i: (i, 0)),
        ],
        out_shape=[
            jax.ShapeDtypeStruct((n, d), F32),
            jax.ShapeDtypeStruct((n, d // 2), I32),
            jax.ShapeDtypeStruct((n, LANES), F32),
        ],
        compiler_params=_cparams(("parallel",)),
        name="merge",
    )(proj, proj, proj, proj, proj, proj, proj, attn, h,
      cw, wc, wa, wm, g2, wr_hi, wr_lo, br)


def _pos_kernel(rt_ref, pos_ref, blk_ref, cnt_ref, run_ref, pst_ref, *, bm):
    ph = pl.program_id(0)
    c = pl.program_id(1)
    tp = rt_ref.shape[0]
    nbp = blk_ref.shape[0]
    lane = lax.broadcasted_iota(I32, (tp, LANES), 1).astype(F32)
    rt = rt_ref[...]
    sel0 = lane == rt[:, 0:1]
    sel1 = lane == rt[:, 1:2]
    both = sel0.astype(F32) + sel1.astype(F32)
    colsum = jnp.sum(both, axis=0, keepdims=True)

    @pl.when((ph == 0) & (c == 0))
    def _():
        cnt_ref[...] = jnp.zeros_like(cnt_ref)

    @pl.when(ph == 0)
    def _():
        cnt_ref[...] += colsum

    @pl.when((ph == 1) & (c == 0))
    def _():
        nblk = jnp.floor((cnt_ref[...] + (bm - 1)) * (1.0 / bm))
        r0 = lax.broadcasted_iota(I32, (LANES, LANES), 0)
        c0 = lax.broadcasted_iota(I32, (LANES, LANES), 1)
        upper = (r0 < c0).astype(BF16)
        nblk8 = jnp.broadcast_to(nblk, (8, LANES)).astype(BF16)
        pstb = jnp.dot(nblk8, upper, preferred_element_type=F32)[0:1, :]
        pst_ref[...] = pstb * bm
        run_ref[...] = jnp.zeros_like(run_ref)
        pend = pstb + nblk
        bidx = lax.broadcasted_iota(I32, (nbp, LANES), 0).astype(F32)
        l2 = lax.broadcasted_iota(I32, (nbp, LANES), 1)
        started = ((pend <= bidx) & (l2 < N_EXPERTS)).astype(F32)
        eid = jnp.minimum(jnp.sum(started, axis=1, keepdims=True), N_EXPERTS - 1.0)
        nused = pend[:, N_EXPERTS - 1:N_EXPERTS]
        valid = (bidx < nused).astype(F32)
        blk_ref[...] = jnp.where(l2 == 0, eid, jnp.where(l2 == 1, valid, 0.0)).astype(I32)

    @pl.when(ph == 1)
    def _():
        r0 = lax.broadcasted_iota(I32, (tp, tp), 0)
        c0 = lax.broadcasted_iota(I32, (tp, tp), 1)
        lower = (c0 < r0).astype(BF16)
        before = jnp.dot(lower, both.astype(BF16), preferred_element_type=F32)
        base = before + run_ref[...] + pst_ref[...]
        d0 = jnp.sum(jnp.where(sel0, base, 0.0), axis=1, keepdims=True)
        d1 = jnp.sum(jnp.where(sel1, base, 0.0), axis=1, keepdims=True)
        pos_ref[...] = jnp.where(lane == 0.0, d0, jnp.where(lane == 1.0, d1, 0.0)).astype(I32)
        run_ref[...] += colsum


def _positions(route, nbp):
    n = route.shape[0]
    tp = TP_POS
    return pl.pallas_call(
        functools.partial(_pos_kernel, bm=BM_EXPERT),
        grid=(2, n // tp),
        in_specs=[pl.BlockSpec((tp, LANES), lambda ph, c: (c, 0))],
        out_specs=[
            pl.BlockSpec((tp, LANES), lambda ph, c: (c * ph, 0)),
            pl.BlockSpec((nbp, LANES), lambda ph, c: (0, 0)),
        ],
        out_shape=[
            jax.ShapeDtypeStruct((n, LANES), I32),
            jax.ShapeDtypeStruct((nbp, LANES), I32),
        ],
        scratch_shapes=[pltpu.VMEM((1, LANES), F32)] * 3,
        compiler_params=_cparams(("arbitrary", "arbitrary")),
        name="positions",
    )(route)


def _dispatch_kernel(d0_ref, d1_ref, xp_hbm, xs_in_hbm, xs_hbm, sem, *, tc):
    del xs_in_hbm
    base = pl.program_id(0) * tc

    def copies(t):
        src = xp_hbm.at[pl.ds(base + t, 1)]
        return (pltpu.make_async_copy(src, xs_hbm.at[pl.ds(d0_ref[t], 1)], sem),
                pltpu.make_async_copy(src, xs_hbm.at[pl.ds(d1_ref[t], 1)], sem))

    def issue(t, carry):
        a, b = copies(t)
        a.start()
        b.start()
        return carry

    def drain(t, carry):
        a, b = copies(t)
        a.wait()
        b.wait()
        return carry

    lax.fori_loop(0, tc, issue, 0)
    lax.fori_loop(0, tc, drain, 0)


def _dispatch(d0, d1, xp, xs_init):
    n = xp.shape[0]
    tc = TC_DISPATCH
    smem = pl.BlockSpec((tc,), lambda i: (i,), memory_space=pltpu.SMEM)
    return pl.pallas_call(
        functools.partial(_dispatch_kernel, tc=tc),
        grid=(n // tc,),
        in_specs=[smem, smem,
                  pl.BlockSpec(memory_space=pl.ANY), pl.BlockSpec(memory_space=pl.ANY)],
        out_specs=pl.BlockSpec(memory_space=pl.ANY),
        out_shape=jax.ShapeDtypeStruct(xs_init.shape, xs_init.dtype),
        scratch_shapes=[pltpu.SemaphoreType.DMA],
        input_output_aliases={3: 0},
        compiler_params=_cparams(("arbitrary",)),
        name="dispatch",
    )(d0, d1, xp, xs_init)


def _expert_kernel(eid_ref, valid_ref, xs_ref, wg_ref, wu_ref, wd_ref, ys_ref):
    b = pl.program_id(0)

    @pl.when(valid_ref[b] != 0)
    def _():
        w = xs_ref[...]
        half = w.shape[1]
        lo = lax.bitcast_convert_type(w << 16, F32).astype(BF16)
        hi = lax.bitcast_convert_type(w & jnp.int32(-65536), F32).astype(BF16)

        def proj(w_ref):
            return (jnp.dot(lo, w_ref[0, :half, :], preferred_element_type=F32)
                    + jnp.dot(hi, w_ref[0, half:, :], preferred_element_type=F32))

        hg = proj(wg_ref)
        hu = proj(wu_ref)
        act = (hg * jax.nn.sigmoid(hg) * hu).astype(BF16)
        ys_ref[...] = jnp.dot(act, wd_ref[0], preferred_element_type=F32)

    @pl.when(valid_ref[b] == 0)
    def _():
        ys_ref[...] = jnp.zeros_like(ys_ref)


def _experts(eid, valid, xs, wg, wu, wd):
    p, half = xs.shape
    d = 2 * half
    de = wg.shape[2]
    bm = BM_EXPERT
    return pl.pallas_call(
        _expert_kernel,
        grid_spec=pltpu.PrefetchScalarGridSpec(
            num_scalar_prefetch=2,
            grid=(p // bm,),
            in_specs=[
                pl.BlockSpec((bm, half), lambda b, e, v: (b, 0)),
                pl.BlockSpec((1, d, de), lambda b, e, v: (e[b], 0, 0)),
                pl.BlockSpec((1, d, de), lambda b, e, v: (e[b], 0, 0)),
                pl.BlockSpec((1, de, d), lambda b, e, v: (e[b], 0, 0)),
            ],
            out_specs=pl.BlockSpec((bm, d), lambda b, e, v: (b, 0)),
        ),
        out_shape=jax.ShapeDtypeStruct((p, d), F32),
        compiler_params=_cparams(("arbitrary",)),
        name="experts",
    )(eid, valid, xs, wg, wu, wd)


def _combine_kernel(d0_ref, d1_ref, h_ref, rt_ref, ys_hbm, o_ref, g0_ref, g1_ref, sem, *, tc):
    def copies(t):
        return (pltpu.make_async_copy(ys_hbm.at[pl.ds(d0_ref[t], 1)], g0_ref.at[pl.ds(t, 1)], sem),
                pltpu.make_async_copy(ys_hbm.at[pl.ds(d1_ref[t], 1)], g1_ref.at[pl.ds(t, 1)], sem))

    def issue(t, carry):
        a, b = copies(t)
        a.start()
        b.start()
        return carry

    def drain(t, carry):
        a, b = copies(t)
        a.wait()
        b.wait()
        return carry

    lax.fori_loop(0, tc, issue, 0)
    lax.fori_loop(0, tc, drain, 0)
    rt = rt_ref[...]
    o_ref[...] = h_ref[...] + rt[:, 2:3] * g0_ref[...] + rt[:, 3:4] * g1_ref[...]


def _combine(d0, d1, h, route, ys):
    n, d = h.shape
    tc = TC_COMBINE
    smem = pl.BlockSpec((tc,), lambda i: (i,), memory_space=pltpu.SMEM)
    return pl.pallas_call(
        functools.partial(_combine_kernel, tc=tc),
        grid=(n // tc,),
        in_specs=[smem, smem,
                  pl.BlockSpec((tc, d), lambda i: (i, 0)),
                  pl.BlockSpec((tc, LANES), lambda i: (i, 0)),
                  pl.BlockSpec(memory_space=pl.ANY)],
        out_specs=pl.BlockSpec((tc, d), lambda i: (i, 0)),
        out_shape=jax.ShapeDtypeStruct((n, d), F32),
        scratch_shapes=[pltpu.VMEM((tc, d), F32), pltpu.VMEM((tc, d), F32),
                        pltpu.SemaphoreType.DMA],
        compiler_params=_cparams(("arbitrary",)),
        name="combine",
    )(d0, d1, h, route, ys)


def _pad_lanes(v, width=LANES):
    return jnp.pad(v, [(0, 0)] * (v.ndim - 1) + [(0, width - v.shape[-1])])


def kernel(x, meta_tokens, norm1_g, w_in, b_forget, conv_w, w_conv_out, q_norm_g, k_norm_g,
           w_att_out, w_merge_out, norm2_g, w_router_group, b_router_group, w_router_expert,
           b_router_expert, w_exp_gate, w_exp_up, w_exp_down):
    batch, seq, d = x.shape
    depth = w_in.shape[0]
    length = seq + N_META
    lp = -(-length // SEQ_ALIGN) * SEQ_ALIGN
    n = batch * lp
    assert d == N_HEADS * HEAD_DIM == 1024
    assert lp % TQ_ATTN == 0 and lp % TM_MERGE == 0 and lp % TR_PREP == 0
    assert n % TM_INPROJ == 0 and n % TC_DISPATCH == 0 and n % TC_COMBINE == 0 and n % TP_POS == 0

    n_blocks = -(-(2 * n + N_EXPERTS * (BM_EXPERT - 1)) // BM_EXPERT)
    nbp = -(-n_blocks // 8) * 8
    p_rows = n_blocks * BM_EXPERT

    meta = jnp.broadcast_to(meta_tokens[None].astype(x.dtype), (batch, N_META, d))
    h = jnp.concatenate([meta, x, jnp.zeros((batch, lp - length, d), x.dtype)], axis=1)
    h = h.reshape(n, d)

    scale = HEAD_DIM ** -0.5
    for l in range(depth):
        w = w_in[l]
        w_all = jnp.concatenate([w[:, :6 * d], w[:, 6 * d + N_HEADS:]], axis=1).astype(BF16)
        w_f = _pad_lanes(w[:, 6 * d:6 * d + N_HEADS]).astype(BF16)
        proj, f = _inproj(h, norm1_g[l][None], w_all, w_f)

        bf = _pad_lanes(b_forget[l][None])
        gq2 = jnp.tile(q_norm_g[l], 2)[None] * (scale * LOG2E)
        gk2 = jnp.tile(k_norm_g[l], 2)[None]
        qa, ka, va = _prep(proj, f, bf, gq2, gk2, batch, lp)
        attn = _attention(qa, ka, va)

        w_r = _pad_lanes(jnp.concatenate([w_router_group[l], w_router_expert[l]], axis=1))
        wr_hi = w_r.astype(BF16)
        wr_lo = (w_r - wr_hi.astype(F32)).astype(BF16)
        b_r = _pad_lanes(jnp.concatenate([b_router_group[l], b_router_expert[l]])[None])
        cw = jnp.pad(conv_w[l], ((0, 8 - conv_w.shape[1]), (0, 0)))
        h_mid, xp, route = _merge(
            proj, attn, h, cw, w_conv_out[l].astype(BF16), w_att_out[l].astype(BF16),
            w_merge_out[l].astype(BF16), norm2_g[l][None], wr_hi, wr_lo, b_r, lp)

        pos, blk = _positions(route, nbp)
        d0, d1 = pos[:, 0], pos[:, 1]
        xs = _dispatch(d0, d1, xp, jnp.zeros((p_rows, d // 2), I32))
        ys = _experts(blk[:n_blocks, 0], blk[:n_blocks, 1], xs,
                      w_exp_gate[l].astype(BF16), w_exp_up[l].astype(BF16),
                      w_exp_down[l].astype(BF16))
        h = _combine(d0, d1, h_mid, route, ys)

    return h.reshape(batch, lp, d)[:, N_META:length]
```

```python
import functools

import jax
import jax.numpy as jnp
from jax import lax
from jax.experimental import pallas as pl
from jax.experimental.pallas import tpu as pltpu
from jax.experimental.pallas import tpu_sc as plsc

F32 = jnp.float32
BF16 = jnp.bfloat16
I32 = jnp.int32

N_META = 16
N_HEADS = 16
HEAD_DIM = 64
N_GROUPS = 4
EXPERTS_PER_GROUP = 8
N_EXPERTS = N_GROUPS * EXPERTS_PER_GROUP
EPS = 1e-6
NEG = -1e30
LOG2E = 1.4426950408889634
BOUND_LANE = HEAD_DIM + 6
SAFE_BOUND_LOG2 = 50.0

LANES = 128
SEQ_ALIGN = 256
ROUTE_LANES = 4 + N_EXPERTS

TM_INPROJ = 768
TR_PREP = 256
TQ_ATTN = 256
TK_ATTN = 256
TM_MERGE = 384
TP_POS = 256
BM_EXPERT = 256
TC_COMBINE = 256

SC_CORES = 2
SC_SUBCORES = 16
SC_WORKERS = SC_CORES * SC_SUBCORES
SC_CHUNK = 48

VMEM_LIMIT = 56 * 1024 * 1024


def _cparams(sem, vmem=VMEM_LIMIT):
    return pltpu.CompilerParams(dimension_semantics=sem, vmem_limit_bytes=vmem)


def _inproj_kernel(x_ref, g_ref, w_ref, wf_ref, o_ref, f_ref, hn_ref):
    @pl.when(pl.program_id(1) == 0)
    def _():
        x = x_ref[...]
        r = lax.rsqrt(jnp.mean(x * x, axis=-1, keepdims=True) + EPS)
        hn = (x * r * g_ref[...]).astype(BF16)
        hn_ref[...] = hn
        f_ref[...] = jnp.dot(hn, wf_ref[...], preferred_element_type=F32)

    o_ref[...] = jnp.dot(hn_ref[...], w_ref[...], preferred_element_type=F32).astype(BF16)


def _inproj(h, g, w_all, w_f):
    n, d = h.shape
    cols = w_all.shape[1]
    tm, tn = TM_INPROJ, 1024
    return pl.pallas_call(
        _inproj_kernel,
        grid=(n // tm, cols // tn),
        in_specs=[
            pl.BlockSpec((tm, d), lambda i, j: (i, 0)),
            pl.BlockSpec((1, d), lambda i, j: (0, 0)),
            pl.BlockSpec((d, tn), lambda i, j: (0, j)),
            pl.BlockSpec((d, LANES), lambda i, j: (0, 0)),
        ],
        out_specs=[
            pl.BlockSpec((tm, tn), lambda i, j: (i, j)),
            pl.BlockSpec((tm, LANES), lambda i, j: (i, 0)),
        ],
        out_shape=[
            jax.ShapeDtypeStruct((n, cols), BF16),
            jax.ShapeDtypeStruct((n, LANES), F32),
        ],
        scratch_shapes=[pltpu.VMEM((tm, d), BF16)],
        compiler_params=_cparams(("parallel", "arbitrary")),
        name="inproj",
    )(h, g, w_all, w_f)


def _split3(c):
    hi = c.astype(BF16).astype(F32)
    r1 = c - hi
    mid = r1.astype(BF16).astype(F32)
    lo = (r1 - mid).astype(BF16).astype(F32)
    return hi, mid, lo


def _prep_kernel(q_ref, k_ref, v_ref, f_ref, bf_ref, gq_ref, gk_ref,
                 qa_ref, ka_ref, va_ref, carry_ref):
    tr = f_ref.shape[0]

    @pl.when(pl.program_id(1) == 0)
    def _():
        carry_ref[...] = jnp.zeros_like(carry_ref)

    f = f_ref[...] + bf_ref[...]
    lf = jnp.minimum(f, 0.0) - jnp.log1p(jnp.exp(-jnp.abs(f)))
    row = lax.broadcasted_iota(I32, (tr, tr), 0)
    col = lax.broadcasted_iota(I32, (tr, tr), 1)
    tril = (col <= row).astype(F32)
    c = jnp.dot(tril, lf, precision=lax.Precision.HIGHEST,
                preferred_element_type=F32) + carry_ref[...]
    carry_ref[...] = c[tr - 1:tr, :]
    c_hi, c_mid, c_lo = _split3(c * LOG2E)

    lane = lax.broadcasted_iota(I32, (tr, LANES), 1)
    low = lane < HEAD_DIM
    ones_q = ((lane >= HEAD_DIM + 3) & (lane < HEAD_DIM + 6)).astype(F32)
    ones_k = (((lane >= HEAD_DIM) & (lane < HEAD_DIM + 3)) | (lane == BOUND_LANE)).astype(F32)
    ones_v = (lane == HEAD_DIM).astype(F32)

    def norm_pair(x, g):
        sq = x * x
        ss_all = jnp.sum(sq, axis=1, keepdims=True)
        ss_lo = jnp.sum(jnp.where(low, sq, 0.0), axis=1, keepdims=True)
        ms = jnp.where(low, ss_lo, ss_all - ss_lo) * (1.0 / HEAD_DIM)
        return x * lax.rsqrt(ms + EPS) * g

    for p in range(N_HEADS // 2):
        sl = slice(p * LANES, (p + 1) * LANES)
        qn = norm_pair(q_ref[:, sl].astype(F32), gq_ref[...])
        kn = norm_pair(k_ref[:, sl].astype(F32), gk_ref[...])
        vv = v_ref[:, sl].astype(F32)
        for par in range(2):
            h = 2 * p + par
            if par == 0:
                qh, kh, vh = qn, kn, vv
            else:
                qh = pltpu.roll(qn, HEAD_DIM, 1)
                kh = pltpu.roll(kn, HEAD_DIM, 1)
                vh = pltpu.roll(vv, HEAD_DIM, 1)
            ch, cm, cl = c_hi[:, h:h + 1], c_mid[:, h:h + 1], c_lo[:, h:h + 1]
            ext_q = jnp.where(lane == HEAD_DIM, ch,
                              jnp.where(lane == HEAD_DIM + 1, cm,
                                        jnp.where(lane == HEAD_DIM + 2, cl, ones_q)))
            ext_k = jnp.where(lane == HEAD_DIM + 3, -ch,
                              jnp.where(lane == HEAD_DIM + 4, -cm,
                                        jnp.where(lane == HEAD_DIM + 5, -cl, ones_k)))
            qa_ref[0, h] = jnp.where(low, qh, ext_q).astype(BF16)
            ka_ref[0, h] = jnp.where(low, kh, ext_k).astype(BF16)
            va_ref[0, h] = jnp.where(low, vh, ones_v).astype(BF16)


def _prep(proj, f, bf, gq2, gk2, batch, lp):
    tr = TR_PREP
    nr = lp // tr
    aug = jax.ShapeDtypeStruct((batch, N_HEADS, lp, LANES), BF16)
    aug_spec = pl.BlockSpec((1, N_HEADS, tr, LANES), lambda b, r: (b, 0, r, 0))
    vec = pl.BlockSpec((1, LANES), lambda b, r: (0, 0))
    return pl.pallas_call(
        _prep_kernel,
        grid=(batch, nr),
        in_specs=[
            pl.BlockSpec((tr, 1024), lambda b, r: (b * nr + r, 3)),
            pl.BlockSpec((tr, 1024), lambda b, r: (b * nr + r, 4)),
            pl.BlockSpec((tr, 1024), lambda b, r: (b * nr + r, 5)),
            pl.BlockSpec((tr, LANES), lambda b, r: (b * nr + r, 0)),
            vec, vec, vec,
        ],
        out_specs=[aug_spec, aug_spec, aug_spec],
        out_shape=[aug, aug, aug],
        scratch_shapes=[pltpu.VMEM((1, LANES), F32)],
        compiler_params=_cparams(("parallel", "arbitrary")),
        name="prep",
    )(proj, proj, proj, f, bf, gq2, gk2)


def _attn_kernel(qa_ref, ka_ref, va_ref, o_ref, acc_ref, kmax_ref, s_ref, *, tq, tk):
    i = pl.program_id(2)
    lp = ka_ref.shape[2]
    assert tq == tk
    n_full = i
    n_diag = 1
    lane = lax.broadcasted_iota(I32, (tq, LANES), 1)
    low = lane < HEAD_DIM
    nt = (((1,), (1,)), ((), ()))

    @pl.when(i == 0)
    def _():
        lk = lax.broadcasted_iota(I32, (tk, LANES), 1) < HEAD_DIM
        for hh in range(2):
            def chunk(r, best, hh=hh):
                kk = ka_ref[0, hh, pl.ds(pl.multiple_of(r * tk, tk), tk), :].astype(F32)
                return jnp.maximum(best, jnp.sum(jnp.where(lk, kk * kk, 0.0), axis=1, keepdims=True))
            best = lax.fori_loop(0, lp // tk, chunk, jnp.zeros((tk, 1), F32))
            kmax_ref[hh] = jnp.broadcast_to(jnp.max(best, axis=0, keepdims=True), (8, LANES))

    def masked_logits(s, j):
        qpos = i * tq + lax.broadcasted_iota(I32, (tq, tk), 0)
        kpos = j * tk + lax.broadcasted_iota(I32, (tq, tk), 1)
        return jnp.where(kpos <= qpos, s, NEG)

    def kv(hh, j):
        start = pl.multiple_of(j * tk, tk)
        return (ka_ref[0, hh, pl.ds(start, tk), :], va_ref[0, hh, pl.ds(start, tk), :])

    qb = []
    worst = jnp.zeros((1, 1), F32)
    for hh in range(2):
        qf = qa_ref[0, hh].astype(F32)
        qn2 = jnp.sum(jnp.where(low, qf * qf, 0.0), axis=1, keepdims=True)
        bound = jnp.sqrt(qn2 * kmax_ref[hh][0:1, 0:1]) + 0.25
        qb.append(jnp.where(lane == BOUND_LANE, -bound, qf).astype(BF16))
        worst = jnp.maximum(worst, jnp.max(bound, axis=0, keepdims=True))
    safe = worst[0, 0] <= SAFE_BOUND_LOG2

    @pl.when(safe)
    def _():
        acc_ref[...] = jnp.zeros_like(acc_ref)

        skew = (lax.broadcasted_iota(I32, (tq, tk), 1) - lax.broadcasted_iota(I32, (tq, tk), 0))
        last = lp // tk - 1

        def logits(j, buf):
            jj = jnp.minimum(j, last)
            for hh in range(2):
                k, _ = kv(hh, jj)
                s = lax.dot_general(qb[hh], k, nt, preferred_element_type=F32)
                s_ref[buf, hh] = jnp.where(skew <= (i - j) * tq, s, NEG)

        def consume(j, buf):
            jj = jnp.minimum(j, last)
            for hh in range(2):
                _, v = kv(hh, jj)
                p = jnp.exp2(s_ref[buf, hh]).astype(BF16)
                acc_ref[hh] += jnp.dot(p, v, preferred_element_type=F32)

        def step(j, cur, nxt):
            logits(j + 1, nxt)
            consume(j, cur)

        def quad(r, carry):
            for u in range(4):
                step(4 * r + u, u % 2, 1 - u % 2)
            return carry

        logits(0, 0)
        quads = i // 4
        lax.fori_loop(0, quads, quad, 0)
        j4 = 4 * quads
        pairs = (i - j4 + 1) // 2

        def pair(r, carry):
            for u in range(2):
                step(j4 + 2 * r + u, u, 1 - u)
            return carry

        lax.fori_loop(0, pairs, pair, 0)
        consume(j4 + 2 * pairs, 0)

    @pl.when(jnp.logical_not(safe))
    def _():
        qs = [qa_ref[0, hh] for hh in range(2)]

        def tile(hh, j, m, acc, masked):
            k, v = kv(hh, j)
            s = lax.dot_general(qs[hh], k, nt, preferred_element_type=F32)
            if masked:
                s = masked_logits(s, j)
            m_new = jnp.maximum(m, jnp.max(s, axis=1, keepdims=True))
            acc = jnp.exp2(m - m_new) * acc + jnp.dot(
                jnp.exp2(s - m_new).astype(BF16), v, preferred_element_type=F32)
            return m_new, acc

        def body(j, carry):
            m0, a0, m1, a1 = carry
            m0, a0 = tile(0, j, m0, a0, False)
            m1, a1 = tile(1, j, m1, a1, False)
            return m0, a0, m1, a1

        m_init = jnp.full((tq, 1), NEG, F32)
        a_init = jnp.zeros((tq, LANES), F32)
        m0, a0, m1, a1 = lax.fori_loop(0, n_full, body, (m_init, a_init, m_init, a_init))
        for d in range(n_diag):
            m0, a0 = tile(0, n_full + d, m0, a0, True)
            m1, a1 = tile(1, n_full + d, m1, a1, True)
        acc_ref[0] = a0
        acc_ref[1] = a1

    a0 = acc_ref[0]
    a1 = acc_ref[1]
    o0 = a0 / a0[:, HEAD_DIM:HEAD_DIM + 1]
    o1 = a1 / a1[:, HEAD_DIM:HEAD_DIM + 1]
    o_ref[...] = jnp.where(low, o0, pltpu.roll(o1, HEAD_DIM, 1)).astype(BF16)


def _attention(qa, ka, va):
    batch, _, lp, _ = qa.shape
    tq, tk = TQ_ATTN, TK_ATTN
    nq = lp // tq
    return pl.pallas_call(
        functools.partial(_attn_kernel, tq=tq, tk=tk),
        grid=(batch, N_HEADS // 2, nq),
        in_specs=[
            pl.BlockSpec((1, 2, tq, LANES), lambda b, h, i: (b, h, i, 0)),
            pl.BlockSpec((1, 2, lp, LANES), lambda b, h, i: (b, h, 0, 0)),
            pl.BlockSpec((1, 2, lp, LANES), lambda b, h, i: (b, h, 0, 0)),
        ],
        out_specs=pl.BlockSpec((tq, LANES), lambda b, h, i: (b * nq + i, h)),
        out_shape=jax.ShapeDtypeStruct((batch * lp, N_HEADS * HEAD_DIM), BF16),
        scratch_shapes=[pltpu.VMEM((2, tq, LANES), F32), pltpu.VMEM((2, 8, LANES), F32),
                        pltpu.VMEM((2, 2, tq, tk), F32)],
        compiler_params=_cparams(("parallel", "parallel", "arbitrary")),
        name="attn",
    )(qa, ka, va)


def _merge_kernel(b_ref, c_ref, u_ref, ch_ref, uh_ref, ga_ref, gb_ref, at_ref, h_ref,
                  cw_ref, wc_ref, wa_ref, wm_ref, g2_ref, wrh_ref, wrl_ref, br_ref,
                  ho_ref, xp_ref, rt_ref, *, tiles_per_seq):
    tm, d = h_ref.shape
    i = pl.program_id(0)

    z = c_ref[...].astype(F32) * u_ref[...].astype(F32)
    hv = (i % tiles_per_seq != 0).astype(F32)
    zh = ch_ref[...].astype(F32) * uh_ref[...].astype(F32) * hv
    hr = zh.shape[0]
    zm1 = zh[hr - 1:hr, :]
    zm2 = zh[hr - 2:hr - 1, :]
    rowid = lax.broadcasted_iota(I32, (tm, d), 0)
    zp1 = jnp.where(rowid == 0, zm1, pltpu.roll(z, 1, 0))
    zp2 = jnp.where(rowid == 0, zm2, jnp.where(rowid == 1, zm1, pltpu.roll(z, 2, 0)))
    cw = cw_ref[...]
    yc = b_ref[...].astype(F32) * (zp2 * cw[0:1, :] + zp1 * cw[1:2, :] + z * cw[2:3, :])

    y_a = jnp.dot(yc.astype(BF16), wc_ref[...], preferred_element_type=F32)
    y_b = jnp.dot(at_ref[...], wa_ref[...], preferred_element_type=F32)
    mix = (jax.nn.sigmoid(ga_ref[...].astype(F32)) * y_a
           + jax.nn.sigmoid(gb_ref[...].astype(F32)) * y_b)
    hnew = h_ref[...] + jnp.dot(mix.astype(BF16), wm_ref[...], preferred_element_type=F32)
    ho_ref[...] = hnew

    r = lax.rsqrt(jnp.mean(hnew * hnew, axis=-1, keepdims=True) + EPS)
    xn = hnew * r * g2_ref[...]
    x_hi = xn.astype(BF16)
    bits = lax.bitcast_convert_type(x_hi.astype(F32), I32)
    half = d // 2
    xp_ref[...] = (lax.shift_right_logical(bits[:, :half], 16)
                   | (bits[:, half:] & jnp.int32(-65536)))

    x_lo = (xn - x_hi.astype(F32)).astype(BF16)
    logits = (jnp.dot(x_hi, wrh_ref[...], preferred_element_type=F32)
              + jnp.dot(x_lo, wrh_ref[...], preferred_element_type=F32)
              + jnp.dot(x_hi, wrl_ref[...], preferred_element_type=F32)) + br_ref[...]

    lanei = lax.broadcasted_iota(I32, (tm, LANES), 1)
    lane = lanei.astype(F32)
    big = jnp.float32(1e9)
    is_g = lanei < N_GROUPS
    is_e = (lanei >= N_GROUPS) & (lanei < ROUTE_LANES)
    gl = jnp.where(is_g, logits, NEG)
    gmax = jnp.max(gl, axis=1, keepdims=True)
    gidx = jnp.min(jnp.where(gl == gmax, lane, big), axis=1, keepdims=True)
    gsum = jnp.sum(jnp.where(is_g, jnp.exp(gl - gmax), 0.0), axis=1, keepdims=True)
    grp = ((lanei - N_GROUPS) >> 3).astype(F32)
    el = jnp.where(is_e & (grp == gidx), logits, NEG)
    emax = jnp.max(el, axis=1, keepdims=True)
    i1 = jnp.min(jnp.where(el == emax, lane, big), axis=1, keepdims=True)
    el2 = jnp.where(lane == i1, NEG, el)
    e2max = jnp.max(el2, axis=1, keepdims=True)
    i2 = jnp.min(jnp.where(el2 == e2max, lane, big), axis=1, keepdims=True)
    t = jnp.exp(e2max - emax)
    w1 = 1.0 / (gsum * (1.0 + t))
    w2 = w1 * t
    rt_ref[...] = jnp.where(lanei == 0, i1 - N_GROUPS,
                            jnp.where(lanei == 1, i2 - N_GROUPS,
                                      jnp.where(lanei == 2, w1,
                                                jnp.where(lanei == 3, w2, 0.0))))


def _merge(proj, attn, h, cw, wc, wa, wm, g2, wr_hi, wr_lo, br, lp):
    n, d = h.shape
    tm = TM_MERGE
    halo = 16
    hb = tm // halo
    row = lambda c: pl.BlockSpec((tm, d), lambda i: (i, c))
    halo_spec = lambda c: pl.BlockSpec((halo, d), lambda i: (jnp.maximum(i * hb - 1, 0), c))
    full = lambda r, c: pl.BlockSpec((r, c), lambda i: (0, 0))
    return pl.pallas_call(
        functools.partial(_merge_kernel, tiles_per_seq=lp // tm),
        grid=(n // tm,),
        in_specs=[
            row(0), row(1), row(2), halo_spec(1), halo_spec(2), row(6), row(7),
            pl.BlockSpec((tm, d), lambda i: (i, 0)),
            pl.BlockSpec((tm, d), lambda i: (i, 0)),
            full(8, d), full(d, d), full(d, d), full(d, d), full(1, d),
            full(d, LANES), full(d, LANES), full(1, LANES),
        ],
        out_specs=[
            pl.BlockSpec((tm, d), lambda i: (i, 0)),
            pl.BlockSpec((tm, d // 2), lambda i: (i, 0)),
            pl.BlockSpec((tm, LANES), lambda i: (i, 0)),
        ],
        out_shape=[
            jax.ShapeDtypeStruct((n, d), F32),
            jax.ShapeDtypeStruct((n, d // 2), I32),
            jax.ShapeDtypeStruct((n, LANES), F32),
        ],
        compiler_params=_cparams(("parallel",)),
        name="merge",
    )(proj, proj, proj, proj, proj, proj, proj, attn, h,
      cw, wc, wa, wm, g2, wr_hi, wr_lo, br)


def _pos_kernel(rt_ref, pos_ref, blk_ref, cnt_ref, run_ref, pst_ref, *, bm):
    ph = pl.program_id(0)
    c = pl.program_id(1)
    tp = rt_ref.shape[0]
    nbp = blk_ref.shape[0]
    lane = lax.broadcasted_iota(I32, (tp, LANES), 1).astype(F32)
    rt = rt_ref[...]
    sel0 = lane == rt[:, 0:1]
    sel1 = lane == rt[:, 1:2]
    both = sel0.astype(F32) + sel1.astype(F32)
    colsum = jnp.sum(both, axis=0, keepdims=True)

    @pl.when((ph == 0) & (c == 0))
    def _():
        cnt_ref[...] = jnp.zeros_like(cnt_ref)

    @pl.when(ph == 0)
    def _():
        cnt_ref[...] += colsum

    @pl.when((ph == 1) & (c == 0))
    def _():
        nblk = jnp.floor((cnt_ref[...] + (bm - 1)) * (1.0 / bm))
        r0 = lax.broadcasted_iota(I32, (LANES, LANES), 0)
        c0 = lax.broadcasted_iota(I32, (LANES, LANES), 1)
        upper = (r0 < c0).astype(BF16)
        nblk8 = jnp.broadcast_to(nblk, (8, LANES)).astype(BF16)
        pstb = jnp.dot(nblk8, upper, preferred_element_type=F32)[0:1, :]
        pst_ref[...] = pstb * bm
        run_ref[...] = jnp.zeros_like(run_ref)
        pend = pstb + nblk
        bidx = lax.broadcasted_iota(I32, (nbp, LANES), 0).astype(F32)
        l2 = lax.broadcasted_iota(I32, (nbp, LANES), 1)
        started = ((pend <= bidx) & (l2 < N_EXPERTS)).astype(F32)
        eid = jnp.minimum(jnp.sum(started, axis=1, keepdims=True), N_EXPERTS - 1.0)
        mine = l2.astype(F32) == eid
        cnt_e = jnp.sum(jnp.where(mine, cnt_ref[...], 0.0), axis=1, keepdims=True)
        first = jnp.sum(jnp.where(mine, pstb, 0.0), axis=1, keepdims=True)
        nvalid = jnp.clip(cnt_e - (bidx[:, 0:1] - first) * bm, 0.0, float(bm))
        blk_ref[...] = jnp.where(l2 == 0, eid, jnp.where(l2 == 1, nvalid, 0.0)).astype(I32)

    @pl.when(ph == 1)
    def _():
        r0 = lax.broadcasted_iota(I32, (tp, tp), 0)
        c0 = lax.broadcasted_iota(I32, (tp, tp), 1)
        lower = (c0 < r0).astype(BF16)
        before = jnp.dot(lower, both.astype(BF16), preferred_element_type=F32)
        base = before + run_ref[...] + pst_ref[...]
        d0 = jnp.sum(jnp.where(sel0, base, 0.0), axis=1, keepdims=True)
        d1 = jnp.sum(jnp.where(sel1, base, 0.0), axis=1, keepdims=True)
        pos_ref[...] = jnp.where(lane == 0.0, d0, jnp.where(lane == 1.0, d1, 0.0)).astype(I32)
        run_ref[...] += colsum


def _positions(route, nbp):
    n = route.shape[0]
    tp = TP_POS
    return pl.pallas_call(
        functools.partial(_pos_kernel, bm=BM_EXPERT),
        grid=(2, n // tp),
        in_specs=[pl.BlockSpec((tp, LANES), lambda ph, c: (c, 0))],
        out_specs=[
            pl.BlockSpec((tp, LANES), lambda ph, c: (c * ph, 0)),
            pl.BlockSpec((nbp, LANES), lambda ph, c: (0, 0)),
        ],
        out_shape=[
            jax.ShapeDtypeStruct((n, LANES), I32),
            jax.ShapeDtypeStruct((nbp, LANES), I32),
        ],
        scratch_shapes=[pltpu.VMEM((1, LANES), F32)] * 3,
        compiler_params=_cparams(("arbitrary", "arbitrary")),
        name="positions",
    )(route)


def _sc_mesh():
    return plsc.VectorSubcoreMesh(core_axis_name="c", subcore_axis_name="s",
                                  num_cores=SC_CORES, num_subcores=SC_SUBCORES)


def _sc_worker_base(rows_per_worker):
    return (lax.axis_index("s") * SC_CORES + lax.axis_index("c")) * rows_per_worker


def _dispatch(dest, xp, p_rows):
    n, half = xp.shape
    per_worker = n // SC_WORKERS
    chunk = SC_CHUNK
    assert n % (8 * SC_WORKERS) == 0 and per_worker % chunk == 0

    def body(xp_hbm, dest_hbm, xs_hbm, i0_v, i1_v, rows_v):
        base = _sc_worker_base(per_worker)

        @pl.loop(0, per_worker // chunk)
        def _(c):
            off = pl.multiple_of(base + c * chunk, 8)
            pltpu.sync_copy(dest_hbm.at[pl.ds(off, chunk)], i0_v)
            pltpu.sync_copy(dest_hbm.at[pl.ds(n + off, chunk)], i1_v)
            pltpu.sync_copy(xp_hbm.at[pl.ds(off, chunk)], rows_v)
            pltpu.sync_copy(rows_v, xs_hbm.at[i0_v])
            pltpu.sync_copy(rows_v, xs_hbm.at[i1_v])

    return pl.kernel(
        body, mesh=_sc_mesh(),
        out_type=jax.ShapeDtypeStruct((p_rows, half), xp.dtype),
        scratch_types=[pltpu.VMEM((chunk,), I32), pltpu.VMEM((chunk,), I32),
                       pltpu.VMEM((chunk, half), xp.dtype)],
        name="dispatch",
    )(xp, dest)


def _gather_rows(ys, dest):
    a_rows = dest.shape[0]
    d = ys.shape[1]
    per_worker = a_rows // SC_WORKERS
    chunk = SC_CHUNK
    assert a_rows % (8 * SC_WORKERS) == 0 and per_worker % chunk == 0

    def body(ys_hbm, dest_hbm, out_hbm, idx_v, rows_v):
        base = _sc_worker_base(per_worker)

        @pl.loop(0, per_worker // chunk)
        def _(c):
            off = pl.multiple_of(base + c * chunk, 8)
            pltpu.sync_copy(dest_hbm.at[pl.ds(off, chunk)], idx_v)
            pltpu.sync_copy(ys_hbm.at[idx_v], rows_v)
            pltpu.sync_copy(rows_v, out_hbm.at[pl.ds(off, chunk)])

    return pl.kernel(
        body, mesh=_sc_mesh(),
        out_type=jax.ShapeDtypeStruct((a_rows, d), ys.dtype),
        scratch_types=[pltpu.VMEM((chunk,), I32), pltpu.VMEM((chunk, d), ys.dtype)],
        name="gather_rows",
    )(ys, dest)


def _expert_kernel(eid_ref, nvalid_ref, xs_ref, wg_ref, wu_ref, wd_ref, ys_ref):
    b = pl.program_id(0)
    nvalid = nvalid_ref[b]

    @pl.when(nvalid != 0)
    def _():
        rows = lax.broadcasted_iota(I32, xs_ref.shape, 0)
        w = jnp.where(rows < nvalid, xs_ref[...], 0)
        half = w.shape[1]
        lo = lax.bitcast_convert_type(w << 16, F32).astype(BF16)
        hi = lax.bitcast_convert_type(w & jnp.int32(-65536), F32).astype(BF16)

        def proj(w_ref):
            return (jnp.dot(lo, w_ref[0, :half, :], preferred_element_type=F32)
                    + jnp.dot(hi, w_ref[0, half:, :], preferred_element_type=F32))

        hg = proj(wg_ref)
        hu = proj(wu_ref)
        act = (hg * jax.nn.sigmoid(hg) * hu).astype(BF16)
        ys_ref[...] = jnp.dot(act, wd_ref[0], preferred_element_type=F32)

    @pl.when(nvalid == 0)
    def _():
        ys_ref[...] = jnp.zeros_like(ys_ref)


def _experts(eid, nvalid, xs, wg, wu, wd):
    p, half = xs.shape
    d = 2 * half
    de = wg.shape[2]
    bm = BM_EXPERT
    return pl.pallas_call(
        _expert_kernel,
        grid_spec=pltpu.PrefetchScalarGridSpec(
            num_scalar_prefetch=2,
            grid=(p // bm,),
            in_specs=[
                pl.BlockSpec((bm, half), lambda b, e, v: (b, 0)),
                pl.BlockSpec((1, d, de), lambda b, e, v: (e[b], 0, 0)),
                pl.BlockSpec((1, d, de), lambda b, e, v: (e[b], 0, 0)),
                pl.BlockSpec((1, de, d), lambda b, e, v: (e[b], 0, 0)),
            ],
            out_specs=pl.BlockSpec((bm, d), lambda b, e, v: (b, 0)),
        ),
        out_shape=jax.ShapeDtypeStruct((p, d), F32),
        compiler_params=_cparams(("arbitrary",)),
        name="experts",
    )(eid, nvalid, xs, wg, wu, wd)


def _combine_kernel(h_ref, rt_ref, g0_ref, g1_ref, o_ref):
    rt = rt_ref[...]
    o_ref[...] = h_ref[...] + rt[:, 2:3] * g0_ref[...] + rt[:, 3:4] * g1_ref[...]


def _combine(h, route, g):
    n, d = h.shape
    tc = TC_COMBINE
    nt = n // tc
    return pl.pallas_call(
        _combine_kernel,
        grid=(nt,),
        in_specs=[pl.BlockSpec((tc, d), lambda i: (i, 0)),
                  pl.BlockSpec((tc, LANES), lambda i: (i, 0)),
                  pl.BlockSpec((tc, d), lambda i: (i, 0)),
                  pl.BlockSpec((tc, d), lambda i: (nt + i, 0))],
        out_specs=pl.BlockSpec((tc, d), lambda i: (i, 0)),
        out_shape=jax.ShapeDtypeStruct((n, d), F32),
        compiler_params=_cparams(("parallel",)),
        name="combine",
    )(h, route, g, g)


def _pad_lanes(v, width=LANES):
    return jnp.pad(v, [(0, 0)] * (v.ndim - 1) + [(0, width - v.shape[-1])])


def kernel(x, meta_tokens, norm1_g, w_in, b_forget, conv_w, w_conv_out, q_norm_g, k_norm_g,
           w_att_out, w_merge_out, norm2_g, w_router_group, b_router_group, w_router_expert,
           b_router_expert, w_exp_gate, w_exp_up, w_exp_down):
    batch, seq, d = x.shape
    depth = w_in.shape[0]
    length = seq + N_META
    lp = -(-length // SEQ_ALIGN) * SEQ_ALIGN
    n = batch * lp
    assert d == N_HEADS * HEAD_DIM == 1024
    assert lp % TQ_ATTN == 0 and lp % TM_MERGE == 0 and lp % TR_PREP == 0
    assert n % TM_INPROJ == 0 and n % TC_COMBINE == 0 and n % TP_POS == 0

    n_blocks = -(-(2 * n + N_EXPERTS * (BM_EXPERT - 1)) // BM_EXPERT)
    nbp = -(-n_blocks // 8) * 8
    p_rows = n_blocks * BM_EXPERT

    meta = jnp.broadcast_to(meta_tokens[None].astype(x.dtype), (batch, N_META, d))
    h = jnp.concatenate([meta, x, jnp.zeros((batch, lp - length, d), x.dtype)], axis=1)
    h = h.reshape(n, d)

    scale = HEAD_DIM ** -0.5
    for l in range(depth):
        w = w_in[l]
        w_all = jnp.concatenate([w[:, :6 * d], w[:, 6 * d + N_HEADS:]], axis=1).astype(BF16)
        w_f = _pad_lanes(w[:, 6 * d:6 * d + N_HEADS]).astype(BF16)
        proj, f = _inproj(h, norm1_g[l][None], w_all, w_f)

        bf = _pad_lanes(b_forget[l][None])
        gq2 = jnp.tile(q_norm_g[l], 2)[None] * (scale * LOG2E)
        gk2 = jnp.tile(k_norm_g[l], 2)[None]
        qa, ka, va = _prep(proj, f, bf, gq2, gk2, batch, lp)
        attn = _attention(qa, ka, va)

        w_r = _pad_lanes(jnp.concatenate([w_router_group[l], w_router_expert[l]], axis=1))
        wr_hi = w_r.astype(BF16)
        wr_lo = (w_r - wr_hi.astype(F32)).astype(BF16)
        b_r = _pad_lanes(jnp.concatenate([b_router_group[l], b_router_expert[l]])[None])
        cw = jnp.pad(conv_w[l], ((0, 8 - conv_w.shape[1]), (0, 0)))
        h_mid, xp, route = _merge(
            proj, attn, h, cw, w_conv_out[l].astype(BF16), w_att_out[l].astype(BF16),
            w_merge_out[l].astype(BF16), norm2_g[l][None], wr_hi, wr_lo, b_r, lp)

        pos, blk = _positions(route, nbp)
        dest = jnp.concatenate([pos[:, 0], pos[:, 1]])
        xs = _dispatch(dest, xp, p_rows)
        ys = _experts(blk[:n_blocks, 0], blk[:n_blocks, 1], xs,
                      w_exp_gate[l].astype(BF16), w_exp_up[l].astype(BF16),
                      w_exp_down[l].astype(BF16))
        h = _combine(h_mid, route, _gather_rows(ys, dest))

    return h.reshape(batch, lp, d)[:, N_META:length]
```

```python
import functools

import jax
import jax.numpy as jnp
from jax import lax
from jax.experimental import pallas as pl
from jax.experimental.pallas import tpu as pltpu
from jax.experimental.pallas import tpu_sc as plsc

F32 = jnp.float32
BF16 = jnp.bfloat16
I32 = jnp.int32

N_META = 16
N_HEADS = 16
HEAD_DIM = 64
N_GROUPS = 4
EXPERTS_PER_GROUP = 8
N_EXPERTS = N_GROUPS * EXPERTS_PER_GROUP
EPS = 1e-6
NEG = -1e30
LOG2E = 1.4426950408889634
BOUND_LANE = HEAD_DIM + 6
SAFE_BOUND_LOG2 = 50.0

LANES = 128
SEQ_ALIGN = 256
ROUTE_LANES = 4 + N_EXPERTS

TM_INPROJ = 1536
TR_PREP = 256
TQ_ATTN = 768
TK_ATTN = 256
TM_MERGE = 384
TP_POS = 256
BM_EXPERT = 256
TC_COMBINE = 256

SC_CORES = 2
SC_SUBCORES = 16
SC_WORKERS = SC_CORES * SC_SUBCORES
SC_CHUNK = 48

VMEM_LIMIT = 56 * 1024 * 1024


def _cparams(sem, vmem=VMEM_LIMIT):
    return pltpu.CompilerParams(dimension_semantics=sem, vmem_limit_bytes=vmem)


def _inproj_kernel(x_ref, g_ref, w_ref, wf_ref, o_ref, f_ref, hn_ref):
    @pl.when(pl.program_id(1) == 0)
    def _():
        x = x_ref[...]
        r = lax.rsqrt(jnp.mean(x * x, axis=-1, keepdims=True) + EPS)
        hn = (x * r * g_ref[...]).astype(BF16)
        hn_ref[...] = hn
        f_ref[...] = jnp.dot(hn, wf_ref[...], preferred_element_type=F32)

    o_ref[...] = jnp.dot(hn_ref[...], w_ref[...], preferred_element_type=F32).astype(BF16)


def _inproj(h, g, w_all, w_f):
    n, d = h.shape
    cols = w_all.shape[1]
    tm, tn = TM_INPROJ, 1024
    return pl.pallas_call(
        _inproj_kernel,
        grid=(n // tm, cols // tn),
        in_specs=[
            pl.BlockSpec((tm, d), lambda i, j: (i, 0)),
            pl.BlockSpec((1, d), lambda i, j: (0, 0)),
            pl.BlockSpec((d, tn), lambda i, j: (0, j)),
            pl.BlockSpec((d, LANES), lambda i, j: (0, 0)),
        ],
        out_specs=[
            pl.BlockSpec((tm, tn), lambda i, j: (i, j)),
            pl.BlockSpec((tm, LANES), lambda i, j: (i, 0)),
        ],
        out_shape=[
            jax.ShapeDtypeStruct((n, cols), BF16),
            jax.ShapeDtypeStruct((n, LANES), F32),
        ],
        scratch_shapes=[pltpu.VMEM((tm, d), BF16)],
        compiler_params=_cparams(("parallel", "arbitrary")),
        name="inproj",
    )(h, g, w_all, w_f)


def _split3(c):
    hi = c.astype(BF16).astype(F32)
    r1 = c - hi
    mid = r1.astype(BF16).astype(F32)
    lo = (r1 - mid).astype(BF16).astype(F32)
    return hi, mid, lo


def _prep_kernel(q_ref, k_ref, v_ref, f_ref, bf_ref, gq_ref, gk_ref, hsel_ref, hexp_ref,
                 selq_ref, selk_ref,
                 qa_ref, ka_ref, va_ref, carry_ref):
    tr = f_ref.shape[0]

    @pl.when(pl.program_id(1) == 0)
    def _():
        carry_ref[...] = jnp.zeros_like(carry_ref)

    f = f_ref[...] + bf_ref[...]
    lf = jnp.minimum(f, 0.0) - jnp.log1p(jnp.exp(-jnp.abs(f)))
    row = lax.broadcasted_iota(I32, (tr, tr), 0)
    col = lax.broadcasted_iota(I32, (tr, tr), 1)
    tril = (col <= row).astype(F32)
    c = jnp.dot(tril, lf, precision=lax.Precision.HIGHEST,
                preferred_element_type=F32) + carry_ref[...]
    carry_ref[...] = c[tr - 1:tr, :]
    c_hi, c_mid, c_lo = _split3(c * LOG2E)

    lane = lax.broadcasted_iota(I32, (tr, LANES), 1)
    low = lane < HEAD_DIM
    ones_v = (lane == HEAD_DIM).astype(F32)

    heads = lane < N_HEADS
    c3 = (jnp.where(heads, c_hi, 0.0)
          + pltpu.roll(jnp.where(heads, c_mid, 0.0), N_HEADS, 1)
          + pltpu.roll(jnp.where(heads, c_lo, 0.0), 2 * N_HEADS, 1)
          + (lane == LANES - 1).astype(F32)).astype(BF16)

    def inv_rms(x):
        ss = jnp.dot((x * x).astype(BF16), hsel_ref[...], preferred_element_type=F32)
        r = lax.rsqrt(ss * (1.0 / HEAD_DIM) + EPS)
        r_hi = r.astype(BF16)
        r_lo = (r - r_hi.astype(F32)).astype(BF16)
        return jnp.dot(jnp.concatenate([r_hi, r_lo], axis=1), hexp_ref[...],
                       preferred_element_type=F32)

    rq = inv_rms(q_ref[...].astype(F32))
    rk = inv_rms(k_ref[...].astype(F32))

    for p in range(N_HEADS // 2):
        sl = slice(p * LANES, (p + 1) * LANES)
        e, o = 2 * p, 2 * p + 1
        qn = q_ref[:, sl].astype(F32) * rq[:, sl] * gq_ref[...]
        kn = k_ref[:, sl].astype(F32) * rk[:, sl] * gk_ref[...]
        vv = v_ref[:, sl].astype(F32)
        pc = slice(2 * p * LANES, (2 * p + 2) * LANES)
        ext_q = jnp.dot(c3, selq_ref[:, pc], preferred_element_type=F32)
        ext_k = jnp.dot(c3, selk_ref[:, pc], preferred_element_type=F32)
        for h in (e, o):
            if h == e:
                qh, kh, vh = qn, kn, vv
            else:
                qh = pltpu.roll(qn, HEAD_DIM, 1)
                kh = pltpu.roll(kn, HEAD_DIM, 1)
                vh = pltpu.roll(vv, HEAD_DIM, 1)
            hs = slice((h - e) * LANES, (h - e + 1) * LANES)
            qa_ref[0, h] = jnp.where(low, qh, ext_q[:, hs]).astype(BF16)
            ka_ref[0, h] = jnp.where(low, kh, ext_k[:, hs]).astype(BF16)
            va_ref[0, h] = jnp.where(low, vh, ones_v).astype(BF16)


def _prep_constants():
    feat = jnp.arange(N_HEADS * HEAD_DIM)
    hsel = (feat[:, None] // HEAD_DIM == jnp.arange(LANES)[None, :]).astype(BF16)
    hexp = jnp.concatenate([hsel.T, hsel.T], axis=0)
    rows = jnp.arange(LANES)
    part, src = rows // N_HEADS, rows % N_HEADS
    cols = jnp.arange(N_HEADS * LANES)
    head, off = cols // LANES, cols % LANES - HEAD_DIM
    is_head = (src[:, None] == head[None, :]) & (part[:, None] < 3)
    const_row = (rows == LANES - 1)[:, None]
    sel_q = (is_head & (off[None, :] == part[:, None])).astype(F32) \
        + (const_row & (off[None, :] >= 3) & (off[None, :] < 6)).astype(F32)
    sel_k = -(is_head & (off[None, :] == part[:, None] + 3)).astype(F32) \
        + (const_row & (((off[None, :] >= 0) & (off[None, :] < 3))
                        | (off[None, :] == BOUND_LANE - HEAD_DIM))).astype(F32)
    return hsel, hexp, sel_q.astype(BF16), sel_k.astype(BF16)


def _prep(proj, f, bf, gq2, gk2, batch, lp):
    tr = TR_PREP
    nr = lp // tr
    hsel, hexp, sel_q, sel_k = _prep_constants()
    aug = jax.ShapeDtypeStruct((batch, N_HEADS, lp, LANES), BF16)
    aug_spec = pl.BlockSpec((1, N_HEADS, tr, LANES), lambda b, r: (b, 0, r, 0))
    vec = pl.BlockSpec((1, LANES), lambda b, r: (0, 0))
    full = lambda a: pl.BlockSpec(a.shape, lambda b, r: (0, 0))
    return pl.pallas_call(
        _prep_kernel,
        grid=(batch, nr),
        in_specs=[
            pl.BlockSpec((tr, 1024), lambda b, r: (b * nr + r, 3)),
            pl.BlockSpec((tr, 1024), lambda b, r: (b * nr + r, 4)),
            pl.BlockSpec((tr, 1024), lambda b, r: (b * nr + r, 5)),
            pl.BlockSpec((tr, LANES), lambda b, r: (b * nr + r, 0)),
            vec, vec, vec, full(hsel), full(hexp), full(sel_q), full(sel_k),
        ],
        out_specs=[aug_spec, aug_spec, aug_spec],
        out_shape=[aug, aug, aug],
        scratch_shapes=[pltpu.VMEM((1, LANES), F32)],
        compiler_params=_cparams(("parallel", "arbitrary")),
        name="prep",
    )(proj, proj, proj, f, bf, gq2, gk2, hsel, hexp, sel_q, sel_k)


def _attn_kernel(qa_ref, ka_ref, va_ref, o_ref, acc_ref, kmax_ref, s_ref, *, tq, tk):
    i = pl.program_id(2)
    lp = ka_ref.shape[2]
    assert tq % tk == 0
    n_diag = tq // tk
    n_full = i * n_diag
    lane = lax.broadcasted_iota(I32, (tq, LANES), 1)
    low = lane < HEAD_DIM
    nt = (((1,), (1,)), ((), ()))

    @pl.when(i == 0)
    def _():
        lk = lax.broadcasted_iota(I32, (tk, LANES), 1) < HEAD_DIM
        for hh in range(2):
            def chunk(r, best, hh=hh):
                kk = ka_ref[0, hh, pl.ds(pl.multiple_of(r * tk, tk), tk), :].astype(F32)
                return jnp.maximum(best, jnp.sum(jnp.where(lk, kk * kk, 0.0), axis=1, keepdims=True))
            best = lax.fori_loop(0, lp // tk, chunk, jnp.zeros((tk, 1), F32))
            kmax_ref[hh] = jnp.broadcast_to(jnp.max(best, axis=0, keepdims=True), (8, LANES))

    def masked_logits(s, j):
        qpos = i * tq + lax.broadcasted_iota(I32, (tq, tk), 0)
        kpos = j * tk + lax.broadcasted_iota(I32, (tq, tk), 1)
        return jnp.where(kpos <= qpos, s, NEG)

    def kv(hh, j):
        start = pl.multiple_of(j * tk, tk)
        return (ka_ref[0, hh, pl.ds(start, tk), :], va_ref[0, hh, pl.ds(start, tk), :])

    qb = []
    worst = jnp.zeros((1, 1), F32)
    for hh in range(2):
        qf = qa_ref[0, hh].astype(F32)
        qn2 = jnp.sum(jnp.where(low, qf * qf, 0.0), axis=1, keepdims=True)
        bound = jnp.sqrt(qn2 * kmax_ref[hh][0:1, 0:1]) + 0.25
        qb.append(jnp.where(lane == BOUND_LANE, -bound, qf).astype(BF16))
        worst = jnp.maximum(worst, jnp.max(bound, axis=0, keepdims=True))
    safe = worst[0, 0] <= SAFE_BOUND_LOG2

    @pl.when(safe)
    def _():
        acc_ref[...] = jnp.zeros_like(acc_ref)

        skew = (lax.broadcasted_iota(I32, (tq, tk), 1) - lax.broadcasted_iota(I32, (tq, tk), 0))
        last = lp // tk - 1

        def logits(j, buf):
            jj = jnp.minimum(j, last)
            for hh in range(2):
                k, _ = kv(hh, jj)
                s = lax.dot_general(qb[hh], k, nt, preferred_element_type=F32)
                s_ref[buf, hh] = jnp.where(skew <= i * tq - j * tk, s, NEG)

        def consume(j, buf):
            jj = jnp.minimum(j, last)
            for hh in range(2):
                _, v = kv(hh, jj)
                p = jnp.exp2(s_ref[buf, hh]).astype(BF16)
                acc_ref[hh] += jnp.dot(p, v, preferred_element_type=F32)

        def step(j, cur, nxt):
            logits(j + 1, nxt)
            consume(j, cur)

        def quad(r, carry):
            for u in range(4):
                step(4 * r + u, u % 2, 1 - u % 2)
            return carry

        n_tiles = n_full + n_diag
        logits(0, 0)
        quads = (n_tiles - 1) // 4
        lax.fori_loop(0, quads, quad, 0)
        j4 = 4 * quads
        pairs = (n_tiles - j4) // 2

        def pair(r, carry):
            for u in range(2):
                step(j4 + 2 * r + u, u, 1 - u)
            return carry

        lax.fori_loop(0, pairs, pair, 0)
        consume(j4 + 2 * pairs, 0)

    @pl.when(jnp.logical_not(safe))
    def _():
        qs = [qa_ref[0, hh] for hh in range(2)]

        def tile(hh, j, m, acc, masked):
            k, v = kv(hh, j)
            s = lax.dot_general(qs[hh], k, nt, preferred_element_type=F32)
            if masked:
                s = masked_logits(s, j)
            m_new = jnp.maximum(m, jnp.max(s, axis=1, keepdims=True))
            acc = jnp.exp2(m - m_new) * acc + jnp.dot(
                jnp.exp2(s - m_new).astype(BF16), v, preferred_element_type=F32)
            return m_new, acc

        def body(j, carry):
            m0, a0, m1, a1 = carry
            m0, a0 = tile(0, j, m0, a0, False)
            m1, a1 = tile(1, j, m1, a1, False)
            return m0, a0, m1, a1

        m_init = jnp.full((tq, 1), NEG, F32)
        a_init = jnp.zeros((tq, LANES), F32)
        m0, a0, m1, a1 = lax.fori_loop(0, n_full, body, (m_init, a_init, m_init, a_init))
        for d in range(n_diag):
            m0, a0 = tile(0, n_full + d, m0, a0, True)
            m1, a1 = tile(1, n_full + d, m1, a1, True)
        acc_ref[0] = a0
        acc_ref[1] = a1

    a0 = acc_ref[0]
    a1 = acc_ref[1]
    o0 = a0 / a0[:, HEAD_DIM:HEAD_DIM + 1]
    o1 = a1 / a1[:, HEAD_DIM:HEAD_DIM + 1]
    o_ref[...] = jnp.where(low, o0, pltpu.roll(o1, HEAD_DIM, 1)).astype(BF16)


def _attention(qa, ka, va):
    batch, _, lp, _ = qa.shape
    tq, tk = TQ_ATTN, TK_ATTN
    nq = lp // tq
    return pl.pallas_call(
        functools.partial(_attn_kernel, tq=tq, tk=tk),
        grid=(batch, N_HEADS // 2, nq),
        in_specs=[
            pl.BlockSpec((1, 2, tq, LANES), lambda b, h, i: (b, h, i, 0)),
            pl.BlockSpec((1, 2, lp, LANES), lambda b, h, i: (b, h, 0, 0)),
            pl.BlockSpec((1, 2, lp, LANES), lambda b, h, i: (b, h, 0, 0)),
        ],
        out_specs=pl.BlockSpec((tq, LANES), lambda b, h, i: (b * nq + i, h)),
        out_shape=jax.ShapeDtypeStruct((batch * lp, N_HEADS * HEAD_DIM), BF16),
        scratch_shapes=[pltpu.VMEM((2, tq, LANES), F32), pltpu.VMEM((2, 8, LANES), F32),
                        pltpu.VMEM((2, 2, tq, tk), F32)],
        compiler_params=_cparams(("parallel", "parallel", "arbitrary")),
        name="attn",
    )(qa, ka, va)


def _merge_kernel(b_ref, c_ref, u_ref, ch_ref, uh_ref, ga_ref, gb_ref, at_ref, h_ref,
                  cw_ref, wc_ref, wa_ref, wm_ref, g2_ref, wrh_ref, wrl_ref, br_ref,
                  ho_ref, xp_ref, rt_ref, *, tiles_per_seq):
    tm, d = h_ref.shape
    i = pl.program_id(0)

    z = c_ref[...].astype(F32) * u_ref[...].astype(F32)
    hv = (i % tiles_per_seq != 0).astype(F32)
    zh = ch_ref[...].astype(F32) * uh_ref[...].astype(F32) * hv
    hr = zh.shape[0]
    zm1 = zh[hr - 1:hr, :]
    zm2 = zh[hr - 2:hr - 1, :]
    rowid = lax.broadcasted_iota(I32, (tm, d), 0)
    zp1 = jnp.where(rowid == 0, zm1, pltpu.roll(z, 1, 0))
    zp2 = jnp.where(rowid == 0, zm2, jnp.where(rowid == 1, zm1, pltpu.roll(z, 2, 0)))
    cw = cw_ref[...]
    yc = b_ref[...].astype(F32) * (zp2 * cw[0:1, :] + zp1 * cw[1:2, :] + z * cw[2:3, :])

    y_a = jnp.dot(yc.astype(BF16), wc_ref[...], preferred_element_type=F32)
    y_b = jnp.dot(at_ref[...], wa_ref[...], preferred_element_type=F32)
    mix = (jax.nn.sigmoid(ga_ref[...].astype(F32)) * y_a
           + jax.nn.sigmoid(gb_ref[...].astype(F32)) * y_b)
    hnew = h_ref[...] + jnp.dot(mix.astype(BF16), wm_ref[...], preferred_element_type=F32)
    ho_ref[...] = hnew

    r = lax.rsqrt(jnp.mean(hnew * hnew, axis=-1, keepdims=True) + EPS)
    xn = hnew * r * g2_ref[...]
    x_hi = xn.astype(BF16)
    bits = lax.bitcast_convert_type(x_hi.astype(F32), I32)
    half = d // 2
    xp_ref[...] = (lax.shift_right_logical(bits[:, :half], 16)
                   | (bits[:, half:] & jnp.int32(-65536)))

    x_lo = (xn - x_hi.astype(F32)).astype(BF16)
    logits = (jnp.dot(x_hi, wrh_ref[...], preferred_element_type=F32)
              + jnp.dot(x_lo, wrh_ref[...], preferred_element_type=F32)
              + jnp.dot(x_hi, wrl_ref[...], preferred_element_type=F32)) + br_ref[...]

    lanei = lax.broadcasted_iota(I32, (tm, LANES), 1)
    lane = lanei.astype(F32)
    big = jnp.float32(1e9)
    is_g = lanei < N_GROUPS
    is_e = (lanei >= N_GROUPS) & (lanei < ROUTE_LANES)
    gl = jnp.where(is_g, logits, NEG)
    gmax = jnp.max(gl, axis=1, keepdims=True)
    gidx = jnp.min(jnp.where(gl == gmax, lane, big), axis=1, keepdims=True)
    gsum = jnp.sum(jnp.where(is_g, jnp.exp(gl - gmax), 0.0), axis=1, keepdims=True)
    grp = ((lanei - N_GROUPS) >> 3).astype(F32)
    el = jnp.where(is_e & (grp == gidx), logits, NEG)
    emax = jnp.max(el, axis=1, keepdims=True)
    i1 = jnp.min(jnp.where(el == emax, lane, big), axis=1, keepdims=True)
    el2 = jnp.where(lane == i1, NEG, el)
    e2max = jnp.max(el2, axis=1, keepdims=True)
    i2 = jnp.min(jnp.where(el2 == e2max, lane, big), axis=1, keepdims=True)
    t = jnp.exp(e2max - emax)
    w1 = 1.0 / (gsum * (1.0 + t))
    w2 = w1 * t
    rt_ref[...] = jnp.where(lanei == 0, i1 - N_GROUPS,
                            jnp.where(lanei == 1, i2 - N_GROUPS,
                                      jnp.where(lanei == 2, w1,
                                                jnp.where(lanei == 3, w2, 0.0))))


def _merge(proj, attn, h, cw, wc, wa, wm, g2, wr_hi, wr_lo, br, lp):
    n, d = h.shape
    tm = TM_MERGE
    halo = 16
    hb = tm // halo
    row = lambda c: pl.BlockSpec((tm, d), lambda i: (i, c))
    halo_spec = lambda c: pl.BlockSpec((halo, d), lambda i: (jnp.maximum(i * hb - 1, 0), c))
    full = lambda r, c: pl.BlockSpec((r, c), lambda i: (0, 0))
    return pl.pallas_call(
        functools.partial(_merge_kernel, tiles_per_seq=lp // tm),
        grid=(n // tm,),
        in_specs=[
            row(0), row(1), row(2), halo_spec(1), halo_spec(2), row(6), row(7),
            pl.BlockSpec((tm, d), lambda i: (i, 0)),
            pl.BlockSpec((tm, d), lambda i: (i, 0)),
            full(8, d), full(d, d), full(d, d), full(d, d), full(1, d),
            full(d, LANES), full(d, LANES), full(1, LANES),
        ],
        out_specs=[
            pl.BlockSpec((tm, d), lambda i: (i, 0)),
            pl.BlockSpec((tm, d // 2), lambda i: (i, 0)),
            pl.BlockSpec((tm, LANES), lambda i: (i, 0)),
        ],
        out_shape=[
            jax.ShapeDtypeStruct((n, d), F32),
            jax.ShapeDtypeStruct((n, d // 2), I32),
            jax.ShapeDtypeStruct((n, LANES), F32),
        ],
        compiler_params=_cparams(("parallel",)),
        name="merge",
    )(proj, proj, proj, proj, proj, proj, proj, attn, h,
      cw, wc, wa, wm, g2, wr_hi, wr_lo, br)


def _pos_kernel(rt_ref, pos_ref, blk_ref, cnt_ref, run_ref, pst_ref, *, bm):
    ph = pl.program_id(0)
    c = pl.program_id(1)
    tp = rt_ref.shape[0]
    nbp = blk_ref.shape[0]
    lane = lax.broadcasted_iota(I32, (tp, LANES), 1).astype(F32)
    rt = rt_ref[...]
    sel0 = lane == rt[:, 0:1]
    sel1 = lane == rt[:, 1:2]
    both = sel0.astype(F32) + sel1.astype(F32)
    colsum = jnp.sum(both, axis=0, keepdims=True)

    @pl.when((ph == 0) & (c == 0))
    def _():
        cnt_ref[...] = jnp.zeros_like(cnt_ref)

    @pl.when(ph == 0)
    def _():
        cnt_ref[...] += colsum

    @pl.when((ph == 1) & (c == 0))
    def _():
        nblk = jnp.floor((cnt_ref[...] + (bm - 1)) * (1.0 / bm))
        r0 = lax.broadcasted_iota(I32, (LANES, LANES), 0)
        c0 = lax.broadcasted_iota(I32, (LANES, LANES), 1)
        upper = (r0 < c0).astype(BF16)
        nblk8 = jnp.broadcast_to(nblk, (8, LANES)).astype(BF16)
        pstb = jnp.dot(nblk8, upper, preferred_element_type=F32)[0:1, :]
        pst_ref[...] = pstb * bm
        run_ref[...] = jnp.zeros_like(run_ref)
        pend = pstb + nblk
        bidx = lax.broadcasted_iota(I32, (nbp, LANES), 0).astype(F32)
        l2 = lax.broadcasted_iota(I32, (nbp, LANES), 1)
        started = ((pend <= bidx) & (l2 < N_EXPERTS)).astype(F32)
        eid = jnp.minimum(jnp.sum(started, axis=1, keepdims=True), N_EXPERTS - 1.0)
        mine = l2.astype(F32) == eid
        cnt_e = jnp.sum(jnp.where(mine, cnt_ref[...], 0.0), axis=1, keepdims=True)
        first = jnp.sum(jnp.where(mine, pstb, 0.0), axis=1, keepdims=True)
        nvalid = jnp.clip(cnt_e - (bidx[:, 0:1] - first) * bm, 0.0, float(bm))
        blk_ref[...] = jnp.where(l2 == 0, eid, jnp.where(l2 == 1, nvalid, 0.0)).astype(I32)

    @pl.when(ph == 1)
    def _():
        r0 = lax.broadcasted_iota(I32, (tp, tp), 0)
        c0 = lax.broadcasted_iota(I32, (tp, tp), 1)
        lower = (c0 < r0).astype(BF16)
        before = jnp.dot(lower, both.astype(BF16), preferred_element_type=F32)
        base = before + run_ref[...] + pst_ref[...]
        d0 = jnp.sum(jnp.where(sel0, base, 0.0), axis=1, keepdims=True)
        d1 = jnp.sum(jnp.where(sel1, base, 0.0), axis=1, keepdims=True)
        pos_ref[...] = jnp.where(lane == 0.0, d0, jnp.where(lane == 1.0, d1, 0.0)).astype(I32)
        run_ref[...] += colsum


def _positions(route, nbp):
    n = route.shape[0]
    tp = TP_POS
    return pl.pallas_call(
        functools.partial(_pos_kernel, bm=BM_EXPERT),
        grid=(2, n // tp),
        in_specs=[pl.BlockSpec((tp, LANES), lambda ph, c: (c, 0))],
        out_specs=[
            pl.BlockSpec((tp, LANES), lambda ph, c: (c * ph, 0)),
            pl.BlockSpec((nbp, LANES), lambda ph, c: (0, 0)),
        ],
        out_shape=[
            jax.ShapeDtypeStruct((n, LANES), I32),
            jax.ShapeDtypeStruct((nbp, LANES), I32),
        ],
        scratch_shapes=[pltpu.VMEM((1, LANES), F32)] * 3,
        compiler_params=_cparams(("arbitrary", "arbitrary")),
        name="positions",
    )(route)


def _sc_mesh():
    return plsc.VectorSubcoreMesh(core_axis_name="c", subcore_axis_name="s",
                                  num_cores=SC_CORES, num_subcores=SC_SUBCORES)


def _sc_worker_base(rows_per_worker):
    return (lax.axis_index("s") * SC_CORES + lax.axis_index("c")) * rows_per_worker


def _dispatch(dest, xp, p_rows):
    n, half = xp.shape
    per_worker = n // SC_WORKERS
    chunk = SC_CHUNK
    assert n % (8 * SC_WORKERS) == 0 and per_worker % chunk == 0

    def body(xp_hbm, dest_hbm, xs_hbm, i0_v, i1_v, rows_v):
        base = _sc_worker_base(per_worker)

        @pl.loop(0, per_worker // chunk)
        def _(c):
            off = pl.multiple_of(base + c * chunk, 8)
            pltpu.sync_copy(dest_hbm.at[pl.ds(off, chunk)], i0_v)
            pltpu.sync_copy(dest_hbm.at[pl.ds(n + off, chunk)], i1_v)
            pltpu.sync_copy(xp_hbm.at[pl.ds(off, chunk)], rows_v)
            pltpu.sync_copy(rows_v, xs_hbm.at[i0_v])
            pltpu.sync_copy(rows_v, xs_hbm.at[i1_v])

    return pl.kernel(
        body, mesh=_sc_mesh(),
        out_type=jax.ShapeDtypeStruct((p_rows, half), xp.dtype),
        scratch_types=[pltpu.VMEM((chunk,), I32), pltpu.VMEM((chunk,), I32),
                       pltpu.VMEM((chunk, half), xp.dtype)],
        name="dispatch",
    )(xp, dest)


def _gather_rows(ys, dest):
    a_rows = dest.shape[0]
    d = ys.shape[1]
    per_worker = a_rows // SC_WORKERS
    chunk = SC_CHUNK
    assert a_rows % (8 * SC_WORKERS) == 0 and per_worker % chunk == 0

    def body(ys_hbm, dest_hbm, out_hbm, idx_v, rows_v):
        base = _sc_worker_base(per_worker)

        @pl.loop(0, per_worker // chunk)
        def _(c):
            off = pl.multiple_of(base + c * chunk, 8)
            pltpu.sync_copy(dest_hbm.at[pl.ds(off, chunk)], idx_v)
            pltpu.sync_copy(ys_hbm.at[idx_v], rows_v)
            pltpu.sync_copy(rows_v, out_hbm.at[pl.ds(off, chunk)])

    return pl.kernel(
        body, mesh=_sc_mesh(),
        out_type=jax.ShapeDtypeStruct((a_rows, d), ys.dtype),
        scratch_types=[pltpu.VMEM((chunk,), I32), pltpu.VMEM((chunk, d), ys.dtype)],
        name="gather_rows",
    )(ys, dest)


def _expert_kernel(eid_ref, nvalid_ref, xs_ref, wg_ref, wu_ref, wd_ref, ys_ref,
                   wgb_ref, wub_ref, wdb_ref):
    b = pl.program_id(0)
    nvalid = nvalid_ref[b]

    @pl.when((b == 0) | (eid_ref[b] != eid_ref[jnp.maximum(b - 1, 0)]))
    def _():
        wgb_ref[...] = wg_ref[...].astype(BF16)
        wub_ref[...] = wu_ref[...].astype(BF16)
        wdb_ref[...] = wd_ref[...].astype(BF16)

    @pl.when(nvalid != 0)
    def _():
        rows = lax.broadcasted_iota(I32, xs_ref.shape, 0)
        w = jnp.where(rows < nvalid, xs_ref[...], 0)
        half = w.shape[1]
        lo = lax.bitcast_convert_type(w << 16, F32).astype(BF16)
        hi = lax.bitcast_convert_type(w & jnp.int32(-65536), F32).astype(BF16)

        def proj(w_ref):
            return (jnp.dot(lo, w_ref[:half, :], preferred_element_type=F32)
                    + jnp.dot(hi, w_ref[half:, :], preferred_element_type=F32))

        hg = proj(wgb_ref)
        hu = proj(wub_ref)
        act = (hg * jax.nn.sigmoid(hg) * hu).astype(BF16)
        ys_ref[...] = jnp.dot(act, wdb_ref[...], preferred_element_type=F32)

    @pl.when(nvalid == 0)
    def _():
        ys_ref[...] = jnp.zeros_like(ys_ref)


def _experts(eid, nvalid, xs, wg, wu, wd, layer):
    p, half = xs.shape
    d = 2 * half
    de = wg.shape[3]
    bm = BM_EXPERT
    return pl.pallas_call(
        _expert_kernel,
        grid_spec=pltpu.PrefetchScalarGridSpec(
            num_scalar_prefetch=2,
            grid=(p // bm,),
            in_specs=[
                pl.BlockSpec((bm, half), lambda b, e, v: (b, 0)),
                pl.BlockSpec((None, None, d, de), lambda b, e, v: (layer, e[b], 0, 0)),
                pl.BlockSpec((None, None, d, de), lambda b, e, v: (layer, e[b], 0, 0)),
                pl.BlockSpec((None, None, de, d), lambda b, e, v: (layer, e[b], 0, 0)),
            ],
            out_specs=pl.BlockSpec((bm, d), lambda b, e, v: (b, 0)),
            scratch_shapes=[pltpu.VMEM((d, de), BF16), pltpu.VMEM((d, de), BF16),
                            pltpu.VMEM((de, d), BF16)],
        ),
        out_shape=jax.ShapeDtypeStruct((p, d), F32),
        compiler_params=_cparams(("arbitrary",)),
        name="experts",
    )(eid, nvalid, xs, wg, wu, wd)


def _combine_kernel(h_ref, rt_ref, g0_ref, g1_ref, o_ref):
    rt = rt_ref[...]
    o_ref[...] = h_ref[...] + rt[:, 2:3] * g0_ref[...] + rt[:, 3:4] * g1_ref[...]


def _combine(h, route, g):
    n, d = h.shape
    tc = TC_COMBINE
    nt = n // tc
    return pl.pallas_call(
        _combine_kernel,
        grid=(nt,),
        in_specs=[pl.BlockSpec((tc, d), lambda i: (i, 0)),
                  pl.BlockSpec((tc, LANES), lambda i: (i, 0)),
                  pl.BlockSpec((tc, d), lambda i: (i, 0)),
                  pl.BlockSpec((tc, d), lambda i: (nt + i, 0))],
        out_specs=pl.BlockSpec((tc, d), lambda i: (i, 0)),
        out_shape=jax.ShapeDtypeStruct((n, d), F32),
        compiler_params=_cparams(("parallel",)),
        name="combine",
    )(h, route, g, g)


def _pad_lanes(v, width=LANES):
    return jnp.pad(v, [(0, 0)] * (v.ndim - 1) + [(0, width - v.shape[-1])])


def kernel(x, meta_tokens, norm1_g, w_in, b_forget, conv_w, w_conv_out, q_norm_g, k_norm_g,
           w_att_out, w_merge_out, norm2_g, w_router_group, b_router_group, w_router_expert,
           b_router_expert, w_exp_gate, w_exp_up, w_exp_down):
    batch, seq, d = x.shape
    depth = w_in.shape[0]
    length = seq + N_META
    lp = -(-length // SEQ_ALIGN) * SEQ_ALIGN
    n = batch * lp
    assert d == N_HEADS * HEAD_DIM == 1024
    assert lp % TQ_ATTN == 0 and lp % TM_MERGE == 0 and lp % TR_PREP == 0
    assert n % TM_INPROJ == 0 and n % TC_COMBINE == 0 and n % TP_POS == 0

    n_blocks = -(-(2 * n + N_EXPERTS * (BM_EXPERT - 1)) // BM_EXPERT)
    nbp = -(-n_blocks // 8) * 8
    p_rows = n_blocks * BM_EXPERT

    meta = jnp.broadcast_to(meta_tokens[None].astype(x.dtype), (batch, N_META, d))
    h = jnp.concatenate([meta, x, jnp.zeros((batch, lp - length, d), x.dtype)], axis=1)
    h = h.reshape(n, d)

    scale = HEAD_DIM ** -0.5
    for l in range(depth):
        w = w_in[l]
        w_all = jnp.concatenate([w[:, :6 * d], w[:, 6 * d + N_HEADS:]], axis=1).astype(BF16)
        w_f = _pad_lanes(w[:, 6 * d:6 * d + N_HEADS]).astype(BF16)
        proj, f = _inproj(h, norm1_g[l][None], w_all, w_f)

        bf = _pad_lanes(b_forget[l][None])
        gq2 = jnp.tile(q_norm_g[l], 2)[None] * (scale * LOG2E)
        gk2 = jnp.tile(k_norm_g[l], 2)[None]
        qa, ka, va = _prep(proj, f, bf, gq2, gk2, batch, lp)
        attn = _attention(qa, ka, va)

        w_r = _pad_lanes(jnp.concatenate([w_router_group[l], w_router_expert[l]], axis=1))
        wr_hi = w_r.astype(BF16)
        wr_lo = (w_r - wr_hi.astype(F32)).astype(BF16)
        b_r = _pad_lanes(jnp.concatenate([b_router_group[l], b_router_expert[l]])[None])
        cw = jnp.pad(conv_w[l], ((0, 8 - conv_w.shape[1]), (0, 0)))
        h_mid, xp, route = _merge(
            proj, attn, h, cw, w_conv_out[l].astype(BF16), w_att_out[l].astype(BF16),
            w_merge_out[l].astype(BF16), norm2_g[l][None], wr_hi, wr_lo, b_r, lp)

        pos, blk = _positions(route, nbp)
        dest = jnp.concatenate([pos[:, 0], pos[:, 1]])
        xs = _dispatch(dest, xp, p_rows)
        ys = _experts(blk[:n_blocks, 0], blk[:n_blocks, 1], xs,
                      w_exp_gate, w_exp_up, w_exp_down, l)
        h = _combine(h_mid, route, _gather_rows(ys, dest))

    return h.reshape(batch, lp, d)[:, N_META:length]
```

```python
import functools

import jax
import jax.numpy as jnp
from jax import lax
from jax.experimental import pallas as pl
from jax.experimental.pallas import tpu as pltpu
from jax.experimental.pallas import tpu_sc as plsc

F32 = jnp.float32
BF16 = jnp.bfloat16
I32 = jnp.int32

N_META = 16
N_HEADS = 16
HEAD_DIM = 64
N_GROUPS = 4
EXPERTS_PER_GROUP = 8
N_EXPERTS = N_GROUPS * EXPERTS_PER_GROUP
EPS = 1e-6
NEG = -1e30
LOG2E = 1.4426950408889634
BOUND_LANE = HEAD_DIM + 6
SAFE_BOUND_LOG2 = 50.0

LANES = 128
SEQ_ALIGN = 256
ROUTE_LANES = 4 + N_EXPERTS

TM_INPROJ = 1536
TR_PREP = 256
TQ_ATTN = 768
TK_ATTN = 256
HEADS_PER_STEP = 4
TM_MERGE = 384
TP_POS = 1536
BM_EXPERT = 256
TC_COMBINE = 256

SC_CORES = 2
SC_SUBCORES = 16
SC_WORKERS = SC_CORES * SC_SUBCORES
SC_CHUNK = 48

VMEM_LIMIT = 56 * 1024 * 1024


def _cparams(sem, vmem=VMEM_LIMIT):
    return pltpu.CompilerParams(dimension_semantics=sem, vmem_limit_bytes=vmem)


def _pack_bf16_halves(x):
    bits = lax.bitcast_convert_type(x.astype(F32), I32)
    half = x.shape[1] // 2
    return lax.shift_right_logical(bits[:, :half], 16) | (bits[:, half:] & jnp.int32(-65536))


def _unpack_bf16_halves(w):
    return (lax.bitcast_convert_type(w << 16, F32),
            lax.bitcast_convert_type(w & jnp.int32(-65536), F32))


def _inproj_kernel(x_ref, g_ref, w_ref, wf_ref, o_ref, f_ref, hn_ref):
    @pl.when(pl.program_id(1) == 0)
    def _():
        x = x_ref[...]
        r = lax.rsqrt(jnp.mean(x * x, axis=-1, keepdims=True) + EPS)
        hn = (x * r * g_ref[...]).astype(BF16)
        hn_ref[...] = hn
        f_ref[...] = jnp.dot(hn, wf_ref[...], preferred_element_type=F32)

    o_ref[...] = jnp.dot(hn_ref[...], w_ref[...], preferred_element_type=F32).astype(BF16)


def _inproj(h, g, w_all, w_f):
    n, d = h.shape
    cols = w_all.shape[1]
    tm, tn = TM_INPROJ, 1024
    return pl.pallas_call(
        _inproj_kernel,
        grid=(n // tm, cols // tn),
        in_specs=[
            pl.BlockSpec((tm, d), lambda i, j: (i, 0)),
            pl.BlockSpec((1, d), lambda i, j: (0, 0)),
            pl.BlockSpec((d, tn), lambda i, j: (0, j)),
            pl.BlockSpec((d, LANES), lambda i, j: (0, 0)),
        ],
        out_specs=[
            pl.BlockSpec((tm, tn), lambda i, j: (i, j)),
            pl.BlockSpec((tm, LANES), lambda i, j: (i, 0)),
        ],
        out_shape=[
            jax.ShapeDtypeStruct((n, cols), BF16),
            jax.ShapeDtypeStruct((n, LANES), F32),
        ],
        scratch_shapes=[pltpu.VMEM((tm, d), BF16)],
        compiler_params=_cparams(("parallel", "arbitrary")),
        name="inproj",
    )(h, g, w_all, w_f)


def _split3(c):
    hi = c.astype(BF16).astype(F32)
    r1 = c - hi
    mid = r1.astype(BF16).astype(F32)
    lo = (r1 - mid).astype(BF16).astype(F32)
    return hi, mid, lo


def _prep_kernel(q_ref, k_ref, v_ref, f_ref, bf_ref, gq_ref, gk_ref, hsel_ref, hexp_ref,
                 selq_ref, selk_ref,
                 qa_ref, ka_ref, va_ref, carry_ref):
    tr = f_ref.shape[0]

    @pl.when(pl.program_id(1) == 0)
    def _():
        carry_ref[...] = jnp.zeros_like(carry_ref)

    f = f_ref[...] + bf_ref[...]
    lf = jnp.minimum(f, 0.0) - jnp.log1p(jnp.exp(-jnp.abs(f)))
    row = lax.broadcasted_iota(I32, (tr, tr), 0)
    col = lax.broadcasted_iota(I32, (tr, tr), 1)
    tril = (col <= row).astype(F32)
    c = jnp.dot(tril, lf, precision=lax.Precision.HIGHEST,
                preferred_element_type=F32) + carry_ref[...]
    carry_ref[...] = c[tr - 1:tr, :]
    c_hi, c_mid, c_lo = _split3(c * LOG2E)

    lane = lax.broadcasted_iota(I32, (tr, LANES), 1)
    low = lane < HEAD_DIM
    ones_v = (lane == HEAD_DIM).astype(F32)

    heads = lane < N_HEADS
    c3 = (jnp.where(heads, c_hi, 0.0)
          + pltpu.roll(jnp.where(heads, c_mid, 0.0), N_HEADS, 1)
          + pltpu.roll(jnp.where(heads, c_lo, 0.0), 2 * N_HEADS, 1)
          + (lane == LANES - 1).astype(F32)).astype(BF16)

    def inv_rms(x):
        ss = jnp.dot((x * x).astype(BF16), hsel_ref[...], preferred_element_type=F32)
        r = lax.rsqrt(ss * (1.0 / HEAD_DIM) + EPS)
        r_hi = r.astype(BF16)
        r_lo = (r - r_hi.astype(F32)).astype(BF16)
        return jnp.dot(jnp.concatenate([r_hi, r_lo], axis=1), hexp_ref[...],
                       preferred_element_type=F32)

    rq = inv_rms(q_ref[...].astype(F32))
    rk = inv_rms(k_ref[...].astype(F32))

    for p in range(N_HEADS // 2):
        sl = slice(p * LANES, (p + 1) * LANES)
        e, o = 2 * p, 2 * p + 1
        qn = q_ref[:, sl].astype(F32) * rq[:, sl] * gq_ref[...]
        kn = k_ref[:, sl].astype(F32) * rk[:, sl] * gk_ref[...]
        vv = v_ref[:, sl].astype(F32)
        pc = slice(2 * p * LANES, (2 * p + 2) * LANES)
        ext_q = jnp.dot(c3, selq_ref[:, pc], preferred_element_type=F32)
        ext_k = jnp.dot(c3, selk_ref[:, pc], preferred_element_type=F32)
        for h in (e, o):
            if h == e:
                qh, kh, vh = qn, kn, vv
            else:
                qh = pltpu.roll(qn, HEAD_DIM, 1)
                kh = pltpu.roll(kn, HEAD_DIM, 1)
                vh = pltpu.roll(vv, HEAD_DIM, 1)
            hs = slice((h - e) * LANES, (h - e + 1) * LANES)
            qa_ref[0, h] = jnp.where(low, qh, ext_q[:, hs]).astype(BF16)
            ka_ref[0, h] = jnp.where(low, kh, ext_k[:, hs]).astype(BF16)
            va_ref[0, h] = jnp.where(low, vh, ones_v).astype(BF16)


def _prep_constants():
    feat = jnp.arange(N_HEADS * HEAD_DIM)
    hsel = (feat[:, None] // HEAD_DIM == jnp.arange(LANES)[None, :]).astype(BF16)
    hexp = jnp.concatenate([hsel.T, hsel.T], axis=0)
    rows = jnp.arange(LANES)
    part, src = rows // N_HEADS, rows % N_HEADS
    cols = jnp.arange(N_HEADS * LANES)
    head, off = cols // LANES, cols % LANES - HEAD_DIM
    is_head = (src[:, None] == head[None, :]) & (part[:, None] < 3)
    const_row = (rows == LANES - 1)[:, None]
    sel_q = (is_head & (off[None, :] == part[:, None])).astype(F32) \
        + (const_row & (off[None, :] >= 3) & (off[None, :] < 6)).astype(F32)
    sel_k = -(is_head & (off[None, :] == part[:, None] + 3)).astype(F32) \
        + (const_row & (((off[None, :] >= 0) & (off[None, :] < 3))
                        | (off[None, :] == BOUND_LANE - HEAD_DIM))).astype(F32)
    return hsel, hexp, sel_q.astype(BF16), sel_k.astype(BF16)


def _prep(proj, f, bf, gq2, gk2, batch, lp):
    tr = TR_PREP
    nr = lp // tr
    hsel, hexp, sel_q, sel_k = _prep_constants()
    aug = jax.ShapeDtypeStruct((batch, N_HEADS, lp, LANES), BF16)
    aug_spec = pl.BlockSpec((1, N_HEADS, tr, LANES), lambda b, r: (b, 0, r, 0))
    vec = pl.BlockSpec((1, LANES), lambda b, r: (0, 0))
    full = lambda a: pl.BlockSpec(a.shape, lambda b, r: (0, 0))
    return pl.pallas_call(
        _prep_kernel,
        grid=(batch, nr),
        in_specs=[
            pl.BlockSpec((tr, 1024), lambda b, r: (b * nr + r, 3)),
            pl.BlockSpec((tr, 1024), lambda b, r: (b * nr + r, 4)),
            pl.BlockSpec((tr, 1024), lambda b, r: (b * nr + r, 5)),
            pl.BlockSpec((tr, LANES), lambda b, r: (b * nr + r, 0)),
            vec, vec, vec, full(hsel), full(hexp), full(sel_q), full(sel_k),
        ],
        out_specs=[aug_spec, aug_spec, aug_spec],
        out_shape=[aug, aug, aug],
        scratch_shapes=[pltpu.VMEM((1, LANES), F32)],
        compiler_params=_cparams(("parallel", "arbitrary")),
        name="prep",
    )(proj, proj, proj, f, bf, gq2, gk2, hsel, hexp, sel_q, sel_k)


def _attn_kernel(qa_ref, ka_ref, va_ref, o_ref, acc_ref, kmax_ref, s_ref, *, tq, tk):
    i = pl.program_id(2)
    nh = qa_ref.shape[1]
    lp = ka_ref.shape[2]
    assert tq % tk == 0 and nh % 2 == 0
    n_diag = tq // tk
    n_full = i * n_diag
    lane = lax.broadcasted_iota(I32, (tq, LANES), 1)
    low = lane < HEAD_DIM
    nt = (((1,), (1,)), ((), ()))

    @pl.when(i == 0)
    def _():
        feat = (lax.broadcasted_iota(I32, (LANES, LANES), 0) < HEAD_DIM).astype(BF16)

        def chunk(r, best):
            out = []
            for hh in range(nh):
                kk = ka_ref[0, hh, pl.ds(pl.multiple_of(r * tk, tk), tk), :]
                n2 = jnp.dot(kk * kk, feat, preferred_element_type=F32)
                out.append(jnp.maximum(best[hh], jnp.max(n2.reshape(tk // 8, 8, LANES), axis=0)))
            return tuple(out)

        best = lax.fori_loop(0, lp // tk, chunk, (jnp.zeros((8, LANES), F32),) * nh)
        for hh in range(nh):
            kmax_ref[hh] = jnp.broadcast_to(1.01 * jnp.max(best[hh], axis=0, keepdims=True),
                                            (8, LANES))

    def kv(hh, j):
        start = pl.multiple_of(j * tk, tk)
        return (ka_ref[0, hh, pl.ds(start, tk), :], va_ref[0, hh, pl.ds(start, tk), :])

    qb = []
    worst = jnp.zeros((1, 1), F32)
    for hh in range(nh):
        qf = qa_ref[0, hh].astype(F32)
        qn2 = jnp.sum(jnp.where(low, qf * qf, 0.0), axis=1, keepdims=True)
        bound = jnp.sqrt(qn2 * kmax_ref[hh][0:1, 0:1]) + 0.25
        qb.append(jnp.where(lane == BOUND_LANE, -bound, qf).astype(BF16))
        worst = jnp.maximum(worst, jnp.max(bound, axis=0, keepdims=True))
    safe = worst[0, 0] <= SAFE_BOUND_LOG2

    @pl.when(safe)
    def _():
        acc_ref[...] = jnp.zeros_like(acc_ref)

        skew = (lax.broadcasted_iota(I32, (tq, tk), 1) - lax.broadcasted_iota(I32, (tq, tk), 0))
        last = lp // tk - 1

        def logits(j, buf):
            jj = jnp.minimum(j, last)
            for hh in range(nh):
                k, _ = kv(hh, jj)
                s = lax.dot_general(qb[hh], k, nt, preferred_element_type=F32)
                s_ref[buf, hh] = jnp.where(skew <= i * tq - j * tk, s, NEG)

        def consume(j, buf):
            jj = jnp.minimum(j, last)
            for hh in range(nh):
                _, v = kv(hh, jj)
                p = jnp.exp2(s_ref[buf, hh]).astype(BF16)
                acc_ref[hh] += jnp.dot(p, v, preferred_element_type=F32)

        def step(j, cur, nxt):
            logits(j + 1, nxt)
            consume(j, cur)

        def quad(r, carry):
            for u in range(4):
                step(4 * r + u, u % 2, 1 - u % 2)
            return carry

        n_tiles = n_full + n_diag
        logits(0, 0)
        quads = (n_tiles - 1) // 4
        lax.fori_loop(0, quads, quad, 0)
        j4 = 4 * quads
        pairs = (n_tiles - j4) // 2

        def pair(r, carry):
            for u in range(2):
                step(j4 + 2 * r + u, u, 1 - u)
            return carry

        lax.fori_loop(0, pairs, pair, 0)
        consume(j4 + 2 * pairs, 0)

    @pl.when(jnp.logical_not(safe))
    def _():
        for hh in range(nh):
            q = qa_ref[0, hh]

            def tile(j, carry, masked, hh=hh, q=q):
                m, acc = carry
                k, v = kv(hh, j)
                s = lax.dot_general(q, k, nt, preferred_element_type=F32)
                if masked:
                    qpos = i * tq + lax.broadcasted_iota(I32, (tq, tk), 0)
                    kpos = j * tk + lax.broadcasted_iota(I32, (tq, tk), 1)
                    s = jnp.where(kpos <= qpos, s, NEG)
                m_new = jnp.maximum(m, jnp.max(s, axis=1, keepdims=True))
                acc = jnp.exp2(m - m_new) * acc + jnp.dot(
                    jnp.exp2(s - m_new).astype(BF16), v, preferred_element_type=F32)
                return m_new, acc

            carry = (jnp.full((tq, 1), NEG, F32), jnp.zeros((tq, LANES), F32))
            carry = lax.fori_loop(0, n_full, functools.partial(tile, masked=False), carry)
            for d in range(n_diag):
                carry = tile(n_full + d, carry, True)
            acc_ref[hh] = carry[1]

    for pr in range(nh // 2):
        a0 = acc_ref[2 * pr]
        a1 = acc_ref[2 * pr + 1]
        o0 = a0 / a0[:, HEAD_DIM:HEAD_DIM + 1]
        o1 = a1 / a1[:, HEAD_DIM:HEAD_DIM + 1]
        o_ref[:, pr * LANES:(pr + 1) * LANES] = jnp.where(
            low, o0, pltpu.roll(o1, HEAD_DIM, 1)).astype(BF16)


def _attention(qa, ka, va):
    batch, _, lp, _ = qa.shape
    tq, tk, nh = TQ_ATTN, TK_ATTN, HEADS_PER_STEP
    nq = lp // tq
    return pl.pallas_call(
        functools.partial(_attn_kernel, tq=tq, tk=tk),
        grid=(batch, N_HEADS // nh, nq),
        in_specs=[
            pl.BlockSpec((1, nh, tq, LANES), lambda b, h, i: (b, h, i, 0)),
            pl.BlockSpec((1, nh, lp, LANES), lambda b, h, i: (b, h, 0, 0)),
            pl.BlockSpec((1, nh, lp, LANES), lambda b, h, i: (b, h, 0, 0)),
        ],
        out_specs=pl.BlockSpec((tq, nh * HEAD_DIM), lambda b, h, i: (b * nq + i, h)),
        out_shape=jax.ShapeDtypeStruct((batch * lp, N_HEADS * HEAD_DIM), BF16),
        scratch_shapes=[pltpu.VMEM((nh, tq, LANES), F32), pltpu.VMEM((nh, 8, LANES), F32),
                        pltpu.VMEM((2, nh, tq, tk), F32)],
        compiler_params=_cparams(("parallel", "parallel", "arbitrary")),
        name="attn",
    )(qa, ka, va)


def _merge_kernel(b_ref, c_ref, u_ref, ch_ref, uh_ref, ga_ref, gb_ref, at_ref, h_ref,
                  cw_ref, wc_ref, wa_ref, wm_ref, g2_ref, wrh_ref, wrl_ref, br_ref,
                  ho_ref, xp_ref, rt_ref, *, tiles_per_seq):
    tm, d = h_ref.shape
    i = pl.program_id(0)

    z = c_ref[...].astype(F32) * u_ref[...].astype(F32)
    hv = (i % tiles_per_seq != 0).astype(F32)
    zh = ch_ref[...].astype(F32) * uh_ref[...].astype(F32) * hv
    hr = zh.shape[0]
    zm1 = zh[hr - 1:hr, :]
    zm2 = zh[hr - 2:hr - 1, :]
    rowid = lax.broadcasted_iota(I32, (tm, d), 0)
    zp1 = jnp.where(rowid == 0, zm1, pltpu.roll(z, 1, 0))
    zp2 = jnp.where(rowid == 0, zm2, jnp.where(rowid == 1, zm1, pltpu.roll(z, 2, 0)))
    cw = cw_ref[...]
    yc = b_ref[...].astype(F32) * (zp2 * cw[0:1, :] + zp1 * cw[1:2, :] + z * cw[2:3, :])

    y_a = jnp.dot(yc.astype(BF16), wc_ref[...], preferred_element_type=F32)
    y_b = jnp.dot(at_ref[...], wa_ref[...], preferred_element_type=F32)
    mix = (jax.nn.sigmoid(ga_ref[...].astype(F32)) * y_a
           + jax.nn.sigmoid(gb_ref[...].astype(F32)) * y_b)
    hnew = h_ref[...] + jnp.dot(mix.astype(BF16), wm_ref[...], preferred_element_type=F32)
    ho_ref[...] = hnew

    r = lax.rsqrt(jnp.mean(hnew * hnew, axis=-1, keepdims=True) + EPS)
    xn = hnew * r * g2_ref[...]
    x_hi = xn.astype(BF16)
    xp_ref[...] = _pack_bf16_halves(x_hi)

    x_lo = (xn - x_hi.astype(F32)).astype(BF16)
    logits = (jnp.dot(x_hi, wrh_ref[...], preferred_element_type=F32)
              + jnp.dot(x_lo, wrh_ref[...], preferred_element_type=F32)
              + jnp.dot(x_hi, wrl_ref[...], preferred_element_type=F32)) + br_ref[...]

    lanei = lax.broadcasted_iota(I32, (tm, LANES), 1)
    lane = lanei.astype(F32)
    big = jnp.float32(1e9)
    is_g = lanei < N_GROUPS
    is_e = (lanei >= N_GROUPS) & (lanei < ROUTE_LANES)
    gl = jnp.where(is_g, logits, NEG)
    gmax = jnp.max(gl, axis=1, keepdims=True)
    gidx = jnp.min(jnp.where(gl == gmax, lane, big), axis=1, keepdims=True)
    gsum = jnp.sum(jnp.where(is_g, jnp.exp(gl - gmax), 0.0), axis=1, keepdims=True)
    grp = ((lanei - N_GROUPS) >> 3).astype(F32)
    el = jnp.where(is_e & (grp == gidx), logits, NEG)
    emax = jnp.max(el, axis=1, keepdims=True)
    i1 = jnp.min(jnp.where(el == emax, lane, big), axis=1, keepdims=True)
    el2 = jnp.where(lane == i1, NEG, el)
    e2max = jnp.max(el2, axis=1, keepdims=True)
    i2 = jnp.min(jnp.where(el2 == e2max, lane, big), axis=1, keepdims=True)
    t = jnp.exp(e2max - emax)
    w1 = 1.0 / (gsum * (1.0 + t))
    w2 = w1 * t
    rt_ref[...] = jnp.where(lanei == 0, i1 - N_GROUPS,
                            jnp.where(lanei == 1, i2 - N_GROUPS,
                                      jnp.where(lanei == 2, w1,
                                                jnp.where(lanei == 3, w2, 0.0))))


def _merge(proj, attn, h, cw, wc, wa, wm, g2, wr_hi, wr_lo, br, lp):
    n, d = h.shape
    tm = TM_MERGE
    halo = 16
    hb = tm // halo
    row = lambda c: pl.BlockSpec((tm, d), lambda i: (i, c))
    halo_spec = lambda c: pl.BlockSpec((halo, d), lambda i: (jnp.maximum(i * hb - 1, 0), c))
    full = lambda r, c: pl.BlockSpec((r, c), lambda i: (0, 0))
    return pl.pallas_call(
        functools.partial(_merge_kernel, tiles_per_seq=lp // tm),
        grid=(n // tm,),
        in_specs=[
            row(0), row(1), row(2), halo_spec(1), halo_spec(2), row(6), row(7),
            pl.BlockSpec((tm, d), lambda i: (i, 0)),
            pl.BlockSpec((tm, d), lambda i: (i, 0)),
            full(8, d), full(d, d), full(d, d), full(d, d), full(1, d),
            full(d, LANES), full(d, LANES), full(1, LANES),
        ],
        out_specs=[
            pl.BlockSpec((tm, d), lambda i: (i, 0)),
            pl.BlockSpec((tm, d // 2), lambda i: (i, 0)),
            pl.BlockSpec((tm, LANES), lambda i: (i, 0)),
        ],
        out_shape=[
            jax.ShapeDtypeStruct((n, d), F32),
            jax.ShapeDtypeStruct((n, d // 2), I32),
            jax.ShapeDtypeStruct((n, LANES), F32),
        ],
        compiler_params=_cparams(("parallel",)),
        name="merge",
    )(proj, proj, proj, proj, proj, proj, proj, attn, h,
      cw, wc, wa, wm, g2, wr_hi, wr_lo, br)


def _pos_kernel(rt_ref, pos_ref, blk_ref, cnt_ref, run_ref, pst_ref, *, bm):
    ph = pl.program_id(0)
    c = pl.program_id(1)
    tp = rt_ref.shape[0]
    nbp = blk_ref.shape[0]
    lane = lax.broadcasted_iota(I32, (tp, LANES), 1).astype(F32)
    rt = rt_ref[...]
    sel0 = lane == rt[:, 0:1]
    sel1 = lane == rt[:, 1:2]
    both = sel0.astype(F32) + sel1.astype(F32)
    colsum = jnp.sum(both, axis=0, keepdims=True)

    @pl.when((ph == 0) & (c == 0))
    def _():
        cnt_ref[...] = jnp.zeros_like(cnt_ref)

    @pl.when(ph == 0)
    def _():
        cnt_ref[...] += colsum

    @pl.when((ph == 1) & (c == 0))
    def _():
        nblk = jnp.floor((cnt_ref[...] + (bm - 1)) * (1.0 / bm))
        r0 = lax.broadcasted_iota(I32, (LANES, LANES), 0)
        c0 = lax.broadcasted_iota(I32, (LANES, LANES), 1)
        upper = (r0 < c0).astype(BF16)
        nblk8 = jnp.broadcast_to(nblk, (8, LANES)).astype(BF16)
        pstb = jnp.dot(nblk8, upper, preferred_element_type=F32)[0:1, :]
        pst_ref[...] = pstb * bm
        run_ref[...] = jnp.zeros_like(run_ref)
        pend = pstb + nblk
        bidx = lax.broadcasted_iota(I32, (nbp, LANES), 0).astype(F32)
        l2 = lax.broadcasted_iota(I32, (nbp, LANES), 1)
        started = ((pend <= bidx) & (l2 < N_EXPERTS)).astype(F32)
        eid = jnp.minimum(jnp.sum(started, axis=1, keepdims=True), N_EXPERTS - 1.0)
        mine = l2.astype(F32) == eid
        cnt_e = jnp.sum(jnp.where(mine, cnt_ref[...], 0.0), axis=1, keepdims=True)
        first = jnp.sum(jnp.where(mine, pstb, 0.0), axis=1, keepdims=True)
        nvalid = jnp.clip(cnt_e - (bidx[:, 0:1] - first) * bm, 0.0, float(bm))
        blk_ref[...] = jnp.where(l2 == 0, eid, jnp.where(l2 == 1, nvalid, 0.0)).astype(I32)

    @pl.when(ph == 1)
    def _():
        r0 = lax.broadcasted_iota(I32, (tp, tp), 0)
        c0 = lax.broadcasted_iota(I32, (tp, tp), 1)
        lower = (c0 < r0).astype(BF16)
        before = jnp.dot(lower, both.astype(BF16), preferred_element_type=F32)
        base = before + run_ref[...] + pst_ref[...]
        d0 = jnp.sum(jnp.where(sel0, base, 0.0), axis=1, keepdims=True)
        d1 = jnp.sum(jnp.where(sel1, base, 0.0), axis=1, keepdims=True)
        pos_ref[...] = jnp.where(lane == 0.0, d0, jnp.where(lane == 1.0, d1, 0.0)).astype(I32)
        run_ref[...] += colsum


def _positions(route, nbp):
    n = route.shape[0]
    tp = TP_POS
    return pl.pallas_call(
        functools.partial(_pos_kernel, bm=BM_EXPERT),
        grid=(2, n // tp),
        in_specs=[pl.BlockSpec((tp, LANES), lambda ph, c: (c, 0))],
        out_specs=[
            pl.BlockSpec((tp, LANES), lambda ph, c: (c * ph, 0)),
            pl.BlockSpec((nbp, LANES), lambda ph, c: (0, 0)),
        ],
        out_shape=[
            jax.ShapeDtypeStruct((n, LANES), I32),
            jax.ShapeDtypeStruct((nbp, LANES), I32),
        ],
        scratch_shapes=[pltpu.VMEM((1, LANES), F32)] * 3,
        compiler_params=_cparams(("arbitrary", "arbitrary")),
        name="positions",
    )(route)


def _sc_mesh():
    return plsc.VectorSubcoreMesh(core_axis_name="c", subcore_axis_name="s",
                                  num_cores=SC_CORES, num_subcores=SC_SUBCORES)


def _sc_worker_base(rows_per_worker):
    return (lax.axis_index("s") * SC_CORES + lax.axis_index("c")) * rows_per_worker


def _dispatch(dest, xp, p_rows):
    n, half = xp.shape
    per_worker = n // SC_WORKERS
    chunk = SC_CHUNK
    assert n % (8 * SC_WORKERS) == 0 and per_worker % chunk == 0

    def body(xp_hbm, dest_hbm, xs_hbm, i0_v, i1_v, rows_v):
        base = _sc_worker_base(per_worker)

        @pl.loop(0, per_worker // chunk)
        def _(c):
            off = pl.multiple_of(base + c * chunk, 8)
            pltpu.sync_copy(dest_hbm.at[pl.ds(off, chunk)], i0_v)
            pltpu.sync_copy(dest_hbm.at[pl.ds(n + off, chunk)], i1_v)
            pltpu.sync_copy(xp_hbm.at[pl.ds(off, chunk)], rows_v)
            pltpu.sync_copy(rows_v, xs_hbm.at[i0_v])
            pltpu.sync_copy(rows_v, xs_hbm.at[i1_v])

    return pl.kernel(
        body, mesh=_sc_mesh(),
        out_type=jax.ShapeDtypeStruct((p_rows, half), xp.dtype),
        scratch_types=[pltpu.VMEM((chunk,), I32), pltpu.VMEM((chunk,), I32),
                       pltpu.VMEM((chunk, half), xp.dtype)],
        name="dispatch",
    )(xp, dest)


def _gather_rows(ys, dest):
    a_rows = dest.shape[0]
    d = ys.shape[1]
    per_worker = a_rows // SC_WORKERS
    chunk = SC_CHUNK
    assert a_rows % (8 * SC_WORKERS) == 0 and per_worker % chunk == 0

    def body(ys_hbm, dest_hbm, out_hbm, idx_v, rows_v):
        base = _sc_worker_base(per_worker)

        @pl.loop(0, per_worker // chunk)
        def _(c):
            off = pl.multiple_of(base + c * chunk, 8)
            pltpu.sync_copy(dest_hbm.at[pl.ds(off, chunk)], idx_v)
            pltpu.sync_copy(ys_hbm.at[idx_v], rows_v)
            pltpu.sync_copy(rows_v, out_hbm.at[pl.ds(off, chunk)])

    return pl.kernel(
        body, mesh=_sc_mesh(),
        out_type=jax.ShapeDtypeStruct((a_rows, d), ys.dtype),
        scratch_types=[pltpu.VMEM((chunk,), I32), pltpu.VMEM((chunk, d), ys.dtype)],
        name="gather_rows",
    )(ys, dest)


def _expert_kernel(eid_ref, nvalid_ref, xs_ref, wg_ref, wu_ref, wd_ref, ys_ref,
                   wgb_ref, wub_ref, wdb_ref):
    b = pl.program_id(0)
    nvalid = nvalid_ref[b]

    @pl.when((b == 0) | (eid_ref[b] != eid_ref[jnp.maximum(b - 1, 0)]))
    def _():
        wgb_ref[...] = wg_ref[...].astype(BF16)
        wub_ref[...] = wu_ref[...].astype(BF16)
        wdb_ref[...] = wd_ref[...].astype(BF16)

    @pl.when(nvalid != 0)
    def _():
        rows = lax.broadcasted_iota(I32, xs_ref.shape, 0)
        w = jnp.where(rows < nvalid, xs_ref[...], 0)
        half = w.shape[1]
        lo, hi = (v.astype(BF16) for v in _unpack_bf16_halves(w))

        def proj(w_ref):
            return (jnp.dot(lo, w_ref[:half, :], preferred_element_type=F32)
                    + jnp.dot(hi, w_ref[half:, :], preferred_element_type=F32))

        hg = proj(wgb_ref)
        hu = proj(wub_ref)
        act = (hg * jax.nn.sigmoid(hg) * hu).astype(BF16)
        y = jnp.dot(act, wdb_ref[...], preferred_element_type=F32)
        ys_ref[...] = _pack_bf16_halves(y.astype(BF16))

    @pl.when(nvalid == 0)
    def _():
        ys_ref[...] = jnp.zeros_like(ys_ref)


def _experts(eid, nvalid, xs, wg, wu, wd, layer):
    p, half = xs.shape
    d = 2 * half
    de = wg.shape[3]
    bm = BM_EXPERT
    return pl.pallas_call(
        _expert_kernel,
        grid_spec=pltpu.PrefetchScalarGridSpec(
            num_scalar_prefetch=2,
            grid=(p // bm,),
            in_specs=[
                pl.BlockSpec((bm, half), lambda b, e, v: (b, 0)),
                pl.BlockSpec((None, None, d, de), lambda b, e, v: (layer, e[b], 0, 0)),
                pl.BlockSpec((None, None, d, de), lambda b, e, v: (layer, e[b], 0, 0)),
                pl.BlockSpec((None, None, de, d), lambda b, e, v: (layer, e[b], 0, 0)),
            ],
            out_specs=pl.BlockSpec((bm, half), lambda b, e, v: (b, 0)),
            scratch_shapes=[pltpu.VMEM((d, de), BF16), pltpu.VMEM((d, de), BF16),
                            pltpu.VMEM((de, d), BF16)],
        ),
        out_shape=jax.ShapeDtypeStruct((p, half), I32),
        compiler_params=_cparams(("arbitrary",)),
        name="experts",
    )(eid, nvalid, xs, wg, wu, wd)


def _combine_kernel(h_ref, rt_ref, g0_ref, g1_ref, o_ref):
    rt = rt_ref[...]
    half = g0_ref.shape[1]
    lo0, hi0 = _unpack_bf16_halves(g0_ref[...])
    lo1, hi1 = _unpack_bf16_halves(g1_ref[...])
    w0, w1 = rt[:, 2:3], rt[:, 3:4]
    o_ref[:, :half] = h_ref[:, :half] + w0 * lo0 + w1 * lo1
    o_ref[:, half:] = h_ref[:, half:] + w0 * hi0 + w1 * hi1


def _combine(h, route, g):
    n, d = h.shape
    half = g.shape[1]
    tc = TC_COMBINE
    nt = n // tc
    return pl.pallas_call(
        _combine_kernel,
        grid=(nt,),
        in_specs=[pl.BlockSpec((tc, d), lambda i: (i, 0)),
                  pl.BlockSpec((tc, LANES), lambda i: (i, 0)),
                  pl.BlockSpec((tc, half), lambda i: (i, 0)),
                  pl.BlockSpec((tc, half), lambda i: (nt + i, 0))],
        out_specs=pl.BlockSpec((tc, d), lambda i: (i, 0)),
        out_shape=jax.ShapeDtypeStruct((n, d), F32),
        compiler_params=_cparams(("parallel",)),
        name="combine",
    )(h, route, g, g)


def _pad_lanes(v, width=LANES):
    return jnp.pad(v, [(0, 0)] * (v.ndim - 1) + [(0, width - v.shape[-1])])


def kernel(x, meta_tokens, norm1_g, w_in, b_forget, conv_w, w_conv_out, q_norm_g, k_norm_g,
           w_att_out, w_merge_out, norm2_g, w_router_group, b_router_group, w_router_expert,
           b_router_expert, w_exp_gate, w_exp_up, w_exp_down):
    batch, seq, d = x.shape
    depth = w_in.shape[0]
    length = seq + N_META
    lp = -(-length // SEQ_ALIGN) * SEQ_ALIGN
    n = batch * lp
    assert d == N_HEADS * HEAD_DIM == 1024
    assert lp % TQ_ATTN == 0 and lp % TM_MERGE == 0 and lp % TR_PREP == 0
    assert n % TM_INPROJ == 0 and n % TC_COMBINE == 0 and n % TP_POS == 0

    n_blocks = -(-(2 * n + N_EXPERTS * (BM_EXPERT - 1)) // BM_EXPERT)
    nbp = -(-n_blocks // 8) * 8
    p_rows = n_blocks * BM_EXPERT

    meta = jnp.broadcast_to(meta_tokens[None].astype(x.dtype), (batch, N_META, d))
    h = jnp.concatenate([meta, x, jnp.zeros((batch, lp - length, d), x.dtype)], axis=1)
    h = h.reshape(n, d)

    scale = HEAD_DIM ** -0.5
    for l in range(depth):
        w = w_in[l]
        w_all = jnp.concatenate([w[:, :6 * d], w[:, 6 * d + N_HEADS:]], axis=1).astype(BF16)
        w_f = _pad_lanes(w[:, 6 * d:6 * d + N_HEADS]).astype(BF16)
        proj, f = _inproj(h, norm1_g[l][None], w_all, w_f)

        bf = _pad_lanes(b_forget[l][None])
        gq2 = jnp.tile(q_norm_g[l], 2)[None] * (scale * LOG2E)
        gk2 = jnp.tile(k_norm_g[l], 2)[None]
        qa, ka, va = _prep(proj, f, bf, gq2, gk2, batch, lp)
        attn = _attention(qa, ka, va)

        w_r = _pad_lanes(jnp.concatenate([w_router_group[l], w_router_expert[l]], axis=1))
        wr_hi = w_r.astype(BF16)
        wr_lo = (w_r - wr_hi.astype(F32)).astype(BF16)
        b_r = _pad_lanes(jnp.concatenate([b_router_group[l], b_router_expert[l]])[None])
        cw = jnp.pad(conv_w[l], ((0, 8 - conv_w.shape[1]), (0, 0)))
        h_mid, xp, route = _merge(
            proj, attn, h, cw, w_conv_out[l].astype(BF16), w_att_out[l].astype(BF16),
            w_merge_out[l].astype(BF16), norm2_g[l][None], wr_hi, wr_lo, b_r, lp)

        pos, blk = _positions(route, nbp)
        dest = jnp.concatenate([pos[:, 0], pos[:, 1]])
        xs = _dispatch(dest, xp, p_rows)
        ys = _experts(blk[:n_blocks, 0], blk[:n_blocks, 1], xs,
                      w_exp_gate, w_exp_up, w_exp_down, l)
        h = _combine(h_mid, route, _gather_rows(ys, dest))

    return h.reshape(batch, lp, d)[:, N_META:length]
```

```python
import functools

import jax
import jax.numpy as jnp
from jax import lax
from jax.experimental import pallas as pl
from jax.experimental.pallas import tpu as pltpu
from jax.experimental.pallas import tpu_sc as plsc

F32 = jnp.float32
BF16 = jnp.bfloat16
I32 = jnp.int32

N_META = 16
N_HEADS = 16
HEAD_DIM = 64
N_GROUPS = 4
EXPERTS_PER_GROUP = 8
N_EXPERTS = N_GROUPS * EXPERTS_PER_GROUP
EPS = 1e-6
NEG = -1e30
LOG2E = 1.4426950408889634
BOUND_LANE = HEAD_DIM + 6
SAFE_BOUND_LOG2 = 50.0

LANES = 128
SEQ_ALIGN = 256
ROUTE_LANES = 4 + N_EXPERTS

TM_INPROJ = 1536
TN_INPROJ = 2048
TR_PREP = 768
TQ_ATTN = 768
TK_ATTN = 256
HEADS_PER_STEP = 4
TM_MERGE = 768
TP_POS = 1536
BM_EXPERT = 512
TC_COMBINE = 1408

SC_CORES = 2
SC_SUBCORES = 16
SC_WORKERS = SC_CORES * SC_SUBCORES
SC_CHUNK = 48

VMEM_LIMIT = 56 * 1024 * 1024


def _cparams(sem, vmem=VMEM_LIMIT):
    return pltpu.CompilerParams(dimension_semantics=sem, vmem_limit_bytes=vmem)


def _pack_bf16_halves(x):
    bits = lax.bitcast_convert_type(x.astype(F32), I32)
    half = x.shape[1] // 2
    return lax.shift_right_logical(bits[:, :half], 16) | (bits[:, half:] & jnp.int32(-65536))


def _unpack_bf16_halves(w):
    return (lax.bitcast_convert_type(w << 16, F32),
            lax.bitcast_convert_type(w & jnp.int32(-65536), F32))


def _inproj_kernel(x_ref, g_ref, w_ref, wf_ref, o_ref, f_ref, hn_ref):
    @pl.when(pl.program_id(1) == 0)
    def _():
        x = x_ref[...]
        r = lax.rsqrt(jnp.mean(x * x, axis=-1, keepdims=True) + EPS)
        hn = (x * r * g_ref[...]).astype(BF16)
        hn_ref[...] = hn
        f_ref[...] = jnp.dot(hn, wf_ref[...], preferred_element_type=F32)

    o_ref[...] = jnp.dot(hn_ref[...], w_ref[...], preferred_element_type=F32).astype(BF16)


def _inproj(h, g, w_all, w_f):
    n, d = h.shape
    cols = w_all.shape[1]
    tm, tn = TM_INPROJ, TN_INPROJ
    return pl.pallas_call(
        _inproj_kernel,
        grid=(n // tm, cols // tn),
        in_specs=[
            pl.BlockSpec((tm, d), lambda i, j: (i, 0)),
            pl.BlockSpec((1, d), lambda i, j: (0, 0)),
            pl.BlockSpec((d, tn), lambda i, j: (0, j)),
            pl.BlockSpec((d, LANES), lambda i, j: (0, 0)),
        ],
        out_specs=[
            pl.BlockSpec((tm, tn), lambda i, j: (i, j)),
            pl.BlockSpec((tm, LANES), lambda i, j: (i, 0)),
        ],
        out_shape=[
            jax.ShapeDtypeStruct((n, cols), BF16),
            jax.ShapeDtypeStruct((n, LANES), F32),
        ],
        scratch_shapes=[pltpu.VMEM((tm, d), BF16)],
        compiler_params=_cparams(("parallel", "arbitrary")),
        name="inproj",
    )(h, g, w_all, w_f)


def _split3(c):
    hi = c.astype(BF16).astype(F32)
    r1 = c - hi
    mid = r1.astype(BF16).astype(F32)
    lo = (r1 - mid).astype(BF16).astype(F32)
    return hi, mid, lo


def _prep_kernel(q_ref, k_ref, v_ref, f_ref, bf_ref, gq_ref, gk_ref, hsel_ref, hexp_ref,
                 selq_ref, selk_ref,
                 qa_ref, ka_ref, va_ref, carry_ref):
    tr = f_ref.shape[0]

    @pl.when(pl.program_id(1) == 0)
    def _():
        carry_ref[...] = jnp.zeros_like(carry_ref)

    f = f_ref[...] + bf_ref[...]
    lf = jnp.minimum(f, 0.0) - jnp.log1p(jnp.exp(-jnp.abs(f)))
    row = lax.broadcasted_iota(I32, (tr, tr), 0)
    col = lax.broadcasted_iota(I32, (tr, tr), 1)
    tril = (col <= row).astype(F32)
    c = jnp.dot(tril, lf, precision=lax.Precision.HIGHEST,
                preferred_element_type=F32) + carry_ref[...]
    carry_ref[...] = c[tr - 1:tr, :]
    c_hi, c_mid, c_lo = _split3(c * LOG2E)

    lane = lax.broadcasted_iota(I32, (tr, LANES), 1)
    low = lane < HEAD_DIM
    ones_v = (lane == HEAD_DIM).astype(F32)

    heads = lane < N_HEADS
    c3 = (jnp.where(heads, c_hi, 0.0)
          + pltpu.roll(jnp.where(heads, c_mid, 0.0), N_HEADS, 1)
          + pltpu.roll(jnp.where(heads, c_lo, 0.0), 2 * N_HEADS, 1)
          + (lane == LANES - 1).astype(F32)).astype(BF16)

    def inv_rms(x):
        ss = jnp.dot((x * x).astype(BF16), hsel_ref[...], preferred_element_type=F32)
        r = lax.rsqrt(ss * (1.0 / HEAD_DIM) + EPS)
        r_hi = r.astype(BF16)
        r_lo = (r - r_hi.astype(F32)).astype(BF16)
        return jnp.dot(jnp.concatenate([r_hi, r_lo], axis=1), hexp_ref[...],
                       preferred_element_type=F32)

    rq = inv_rms(q_ref[...].astype(F32))
    rk = inv_rms(k_ref[...].astype(F32))

    for p in range(N_HEADS // 2):
        sl = slice(p * LANES, (p + 1) * LANES)
        e, o = 2 * p, 2 * p + 1
        qn = q_ref[:, sl].astype(F32) * rq[:, sl] * gq_ref[...]
        kn = k_ref[:, sl].astype(F32) * rk[:, sl] * gk_ref[...]
        vv = v_ref[:, sl].astype(F32)
        pc = slice(2 * p * LANES, (2 * p + 2) * LANES)
        ext_q = jnp.dot(c3, selq_ref[:, pc], preferred_element_type=F32)
        ext_k = jnp.dot(c3, selk_ref[:, pc], preferred_element_type=F32)
        for h in (e, o):
            if h == e:
                qh, kh, vh = qn, kn, vv
            else:
                qh = pltpu.roll(qn, HEAD_DIM, 1)
                kh = pltpu.roll(kn, HEAD_DIM, 1)
                vh = pltpu.roll(vv, HEAD_DIM, 1)
            hs = slice((h - e) * LANES, (h - e + 1) * LANES)
            qa_ref[0, h] = jnp.where(low, qh, ext_q[:, hs]).astype(BF16)
            ka_ref[0, h] = jnp.where(low, kh, ext_k[:, hs]).astype(BF16)
            va_ref[0, h] = jnp.where(low, vh, ones_v).astype(BF16)


def _prep_constants():
    feat = jnp.arange(N_HEADS * HEAD_DIM)
    hsel = (feat[:, None] // HEAD_DIM == jnp.arange(LANES)[None, :]).astype(BF16)
    hexp = jnp.concatenate([hsel.T, hsel.T], axis=0)
    rows = jnp.arange(LANES)
    part, src = rows // N_HEADS, rows % N_HEADS
    cols = jnp.arange(N_HEADS * LANES)
    head, off = cols // LANES, cols % LANES - HEAD_DIM
    is_head = (src[:, None] == head[None, :]) & (part[:, None] < 3)
    const_row = (rows == LANES - 1)[:, None]
    sel_q = (is_head & (off[None, :] == part[:, None])).astype(F32) \
        + (const_row & (off[None, :] >= 3) & (off[None, :] < 6)).astype(F32)
    sel_k = -(is_head & (off[None, :] == part[:, None] + 3)).astype(F32) \
        + (const_row & (((off[None, :] >= 0) & (off[None, :] < 3))
                        | (off[None, :] == BOUND_LANE - HEAD_DIM))).astype(F32)
    return hsel, hexp, sel_q.astype(BF16), sel_k.astype(BF16)


def _prep(proj, f, bf, gq2, gk2, batch, lp):
    tr = TR_PREP
    nr = lp // tr
    hsel, hexp, sel_q, sel_k = _prep_constants()
    aug = jax.ShapeDtypeStruct((batch, N_HEADS, lp, LANES), BF16)
    aug_spec = pl.BlockSpec((1, N_HEADS, tr, LANES), lambda b, r: (b, 0, r, 0))
    vec = pl.BlockSpec((1, LANES), lambda b, r: (0, 0))
    full = lambda a: pl.BlockSpec(a.shape, lambda b, r: (0, 0))
    return pl.pallas_call(
        _prep_kernel,
        grid=(batch, nr),
        in_specs=[
            pl.BlockSpec((tr, 1024), lambda b, r: (b * nr + r, 3)),
            pl.BlockSpec((tr, 1024), lambda b, r: (b * nr + r, 4)),
            pl.BlockSpec((tr, 1024), lambda b, r: (b * nr + r, 5)),
            pl.BlockSpec((tr, LANES), lambda b, r: (b * nr + r, 0)),
            vec, vec, vec, full(hsel), full(hexp), full(sel_q), full(sel_k),
        ],
        out_specs=[aug_spec, aug_spec, aug_spec],
        out_shape=[aug, aug, aug],
        scratch_shapes=[pltpu.VMEM((1, LANES), F32)],
        compiler_params=_cparams(("parallel", "arbitrary")),
        name="prep",
    )(proj, proj, proj, f, bf, gq2, gk2, hsel, hexp, sel_q, sel_k)


def _attn_kernel(qa_ref, ka_ref, va_ref, o_ref, acc_ref, kmax_ref, s_ref, *, tq, tk):
    i = pl.program_id(2)
    nh = qa_ref.shape[1]
    lp = ka_ref.shape[2]
    assert tq % tk == 0 and nh % 2 == 0
    n_diag = tq // tk
    n_full = i * n_diag
    lane = lax.broadcasted_iota(I32, (tq, LANES), 1)
    low = lane < HEAD_DIM
    nt = (((1,), (1,)), ((), ()))

    @pl.when(i == 0)
    def _():
        feat = (lax.broadcasted_iota(I32, (LANES, LANES), 0) < HEAD_DIM).astype(BF16)

        def chunk(r, best):
            out = []
            for hh in range(nh):
                kk = ka_ref[0, hh, pl.ds(pl.multiple_of(r * tk, tk), tk), :]
                n2 = jnp.dot(kk * kk, feat, preferred_element_type=F32)
                out.append(jnp.maximum(best[hh], jnp.max(n2.reshape(tk // 8, 8, LANES), axis=0)))
            return tuple(out)

        best = lax.fori_loop(0, lp // tk, chunk, (jnp.zeros((8, LANES), F32),) * nh)
        for hh in range(nh):
            kmax_ref[hh] = jnp.broadcast_to(1.01 * jnp.max(best[hh], axis=0, keepdims=True),
                                            (8, LANES))

    def kv(hh, j):
        start = pl.multiple_of(j * tk, tk)
        return (ka_ref[0, hh, pl.ds(start, tk), :], va_ref[0, hh, pl.ds(start, tk), :])

    qb = []
    worst = jnp.zeros((1, 1), F32)
    feat_q = (lax.broadcasted_iota(I32, (LANES, LANES), 0) < HEAD_DIM).astype(BF16)
    for hh in range(nh):
        q = qa_ref[0, hh]
        qn2 = 1.01 * jnp.dot(q * q, feat_q, preferred_element_type=F32)
        bound = jnp.sqrt(qn2 * kmax_ref[hh][0:1, :]) + 0.25
        qb.append(jnp.where(lane == BOUND_LANE, -bound, q.astype(F32)).astype(BF16))
        worst = jnp.maximum(worst, jnp.max(bound, axis=0, keepdims=True)[:, 0:1])
    safe = worst[0, 0] <= SAFE_BOUND_LOG2

    @pl.when(safe)
    def _():
        acc_ref[...] = jnp.zeros_like(acc_ref)

        skew = (lax.broadcasted_iota(I32, (tq, tk), 1) - lax.broadcasted_iota(I32, (tq, tk), 0))
        last = lp // tk - 1

        def logits(j, buf):
            jj = jnp.minimum(j, last)
            for hh in range(nh):
                k, _ = kv(hh, jj)
                s = lax.dot_general(qb[hh], k, nt, preferred_element_type=F32)
                s_ref[buf, hh] = jnp.where(skew <= i * tq - j * tk, s, NEG)

        def consume(j, buf):
            jj = jnp.minimum(j, last)
            for hh in range(nh):
                _, v = kv(hh, jj)
                p = jnp.exp2(s_ref[buf, hh]).astype(BF16)
                acc_ref[hh] += jnp.dot(p, v, preferred_element_type=F32)

        def step(j, cur, nxt):
            logits(j + 1, nxt)
            consume(j, cur)

        def quad(r, carry):
            for u in range(4):
                step(4 * r + u, u % 2, 1 - u % 2)
            return carry

        n_tiles = n_full + n_diag
        logits(0, 0)
        quads = (n_tiles - 1) // 4
        lax.fori_loop(0, quads, quad, 0)
        j4 = 4 * quads
        pairs = (n_tiles - j4) // 2

        def pair(r, carry):
            for u in range(2):
                step(j4 + 2 * r + u, u, 1 - u)
            return carry

        lax.fori_loop(0, pairs, pair, 0)
        consume(j4 + 2 * pairs, 0)

    @pl.when(jnp.logical_not(safe))
    def _():
        for hh in range(nh):
            q = qa_ref[0, hh]

            def tile(j, carry, masked, hh=hh, q=q):
                m, acc = carry
                k, v = kv(hh, j)
                s = lax.dot_general(q, k, nt, preferred_element_type=F32)
                if masked:
                    qpos = i * tq + lax.broadcasted_iota(I32, (tq, tk), 0)
                    kpos = j * tk + lax.broadcasted_iota(I32, (tq, tk), 1)
                    s = jnp.where(kpos <= qpos, s, NEG)
                m_new = jnp.maximum(m, jnp.max(s, axis=1, keepdims=True))
                acc = jnp.exp2(m - m_new) * acc + jnp.dot(
                    jnp.exp2(s - m_new).astype(BF16), v, preferred_element_type=F32)
                return m_new, acc

            carry = (jnp.full((tq, 1), NEG, F32), jnp.zeros((tq, LANES), F32))
            carry = lax.fori_loop(0, n_full, functools.partial(tile, masked=False), carry)
            for d in range(n_diag):
                carry = tile(n_full + d, carry, True)
            acc_ref[hh] = carry[1]

    for pr in range(nh // 2):
        a0 = acc_ref[2 * pr]
        a1 = acc_ref[2 * pr + 1]
        o0 = a0 / a0[:, HEAD_DIM:HEAD_DIM + 1]
        o1 = a1 / a1[:, HEAD_DIM:HEAD_DIM + 1]
        o_ref[:, pr * LANES:(pr + 1) * LANES] = jnp.where(
            low, o0, pltpu.roll(o1, HEAD_DIM, 1)).astype(BF16)


def _attention(qa, ka, va):
    batch, _, lp, _ = qa.shape
    tq, tk, nh = TQ_ATTN, TK_ATTN, HEADS_PER_STEP
    nq = lp // tq
    return pl.pallas_call(
        functools.partial(_attn_kernel, tq=tq, tk=tk),
        grid=(batch, N_HEADS // nh, nq),
        in_specs=[
            pl.BlockSpec((1, nh, tq, LANES), lambda b, h, i: (b, h, i, 0)),
            pl.BlockSpec((1, nh, lp, LANES), lambda b, h, i: (b, h, 0, 0)),
            pl.BlockSpec((1, nh, lp, LANES), lambda b, h, i: (b, h, 0, 0)),
        ],
        out_specs=pl.BlockSpec((tq, nh * HEAD_DIM), lambda b, h, i: (b * nq + i, h)),
        out_shape=jax.ShapeDtypeStruct((batch * lp, N_HEADS * HEAD_DIM), BF16),
        scratch_shapes=[pltpu.VMEM((nh, tq, LANES), F32), pltpu.VMEM((nh, 8, LANES), F32),
                        pltpu.VMEM((2, nh, tq, tk), F32)],
        compiler_params=_cparams(("parallel", "parallel", "arbitrary")),
        name="attn",
    )(qa, ka, va)


def _merge_kernel(b_ref, c_ref, u_ref, ch_ref, uh_ref, ga_ref, gb_ref, at_ref, h_ref,
                  cw_ref, wc_ref, wa_ref, wm_ref, g2_ref, wrh_ref, wrl_ref, br_ref,
                  ho_ref, xp_ref, rt_ref, *, tiles_per_seq):
    tm, d = h_ref.shape
    i = pl.program_id(0)

    z = c_ref[...].astype(F32) * u_ref[...].astype(F32)
    hv = (i % tiles_per_seq != 0).astype(F32)
    zh = ch_ref[...].astype(F32) * uh_ref[...].astype(F32) * hv
    hr = zh.shape[0]
    zm1 = zh[hr - 1:hr, :]
    zm2 = zh[hr - 2:hr - 1, :]
    rowid = lax.broadcasted_iota(I32, (tm, d), 0)
    zp1 = jnp.where(rowid == 0, zm1, pltpu.roll(z, 1, 0))
    zp2 = jnp.where(rowid == 0, zm2, jnp.where(rowid == 1, zm1, pltpu.roll(z, 2, 0)))
    cw = cw_ref[...]
    yc = b_ref[...].astype(F32) * (zp2 * cw[0:1, :] + zp1 * cw[1:2, :] + z * cw[2:3, :])

    y_a = jnp.dot(yc.astype(BF16), wc_ref[...], preferred_element_type=F32)
    y_b = jnp.dot(at_ref[...], wa_ref[...], preferred_element_type=F32)
    mix = (jax.nn.sigmoid(ga_ref[...].astype(F32)) * y_a
           + jax.nn.sigmoid(gb_ref[...].astype(F32)) * y_b)
    hnew = h_ref[...] + jnp.dot(mix.astype(BF16), wm_ref[...], preferred_element_type=F32)
    ho_ref[...] = hnew

    r = lax.rsqrt(jnp.mean(hnew * hnew, axis=-1, keepdims=True) + EPS)
    xn = hnew * r * g2_ref[...]
    x_hi = xn.astype(BF16)
    xp_ref[...] = _pack_bf16_halves(x_hi)

    x_lo = (xn - x_hi.astype(F32)).astype(BF16)
    logits = (jnp.dot(x_hi, wrh_ref[...], preferred_element_type=F32)
              + jnp.dot(x_lo, wrh_ref[...], preferred_element_type=F32)
              + jnp.dot(x_hi, wrl_ref[...], preferred_element_type=F32)) + br_ref[...]

    lanei = lax.broadcasted_iota(I32, (tm, LANES), 1)
    lane = lanei.astype(F32)
    big = jnp.float32(1e9)
    is_g = lanei < N_GROUPS
    is_e = (lanei >= N_GROUPS) & (lanei < ROUTE_LANES)
    gl = jnp.where(is_g, logits, NEG)
    gmax = jnp.max(gl, axis=1, keepdims=True)
    gidx = jnp.min(jnp.where(gl == gmax, lane, big), axis=1, keepdims=True)
    gsum = jnp.sum(jnp.where(is_g, jnp.exp(gl - gmax), 0.0), axis=1, keepdims=True)
    grp = ((lanei - N_GROUPS) >> 3).astype(F32)
    el = jnp.where(is_e & (grp == gidx), logits, NEG)
    emax = jnp.max(el, axis=1, keepdims=True)
    i1 = jnp.min(jnp.where(el == emax, lane, big), axis=1, keepdims=True)
    el2 = jnp.where(lane == i1, NEG, el)
    e2max = jnp.max(el2, axis=1, keepdims=True)
    i2 = jnp.min(jnp.where(el2 == e2max, lane, big), axis=1, keepdims=True)
    t = jnp.exp(e2max - emax)
    w1 = 1.0 / (gsum * (1.0 + t))
    w2 = w1 * t
    rt_ref[...] = jnp.where(lanei == 0, i1 - N_GROUPS,
                            jnp.where(lanei == 1, i2 - N_GROUPS,
                                      jnp.where(lanei == 2, w1,
                                                jnp.where(lanei == 3, w2, 0.0))))


def _merge(proj, attn, h, cw, wc, wa, wm, g2, wr_hi, wr_lo, br, lp):
    n, d = h.shape
    tm = TM_MERGE
    halo = 16
    hb = tm // halo
    row = lambda c: pl.BlockSpec((tm, d), lambda i: (i, c))
    halo_spec = lambda c: pl.BlockSpec((halo, d), lambda i: (jnp.maximum(i * hb - 1, 0), c))
    full = lambda r, c: pl.BlockSpec((r, c), lambda i: (0, 0))
    return pl.pallas_call(
        functools.partial(_merge_kernel, tiles_per_seq=lp // tm),
        grid=(n // tm,),
        in_specs=[
            row(0), row(1), row(2), halo_spec(1), halo_spec(2), row(6), row(7),
            pl.BlockSpec((tm, d), lambda i: (i, 0)),
            pl.BlockSpec((tm, d), lambda i: (i, 0)),
            full(8, d), full(d, d), full(d, d), full(d, d), full(1, d),
            full(d, LANES), full(d, LANES), full(1, LANES),
        ],
        out_specs=[
            pl.BlockSpec((tm, d), lambda i: (i, 0)),
            pl.BlockSpec((tm, d // 2), lambda i: (i, 0)),
            pl.BlockSpec((tm, LANES), lambda i: (i, 0)),
        ],
        out_shape=[
            jax.ShapeDtypeStruct((n, d), F32),
            jax.ShapeDtypeStruct((n, d // 2), I32),
            jax.ShapeDtypeStruct((n, LANES), F32),
        ],
        compiler_params=_cparams(("parallel",)),
        name="merge",
    )(proj, proj, proj, proj, proj, proj, proj, attn, h,
      cw, wc, wa, wm, g2, wr_hi, wr_lo, br)


def _pos_kernel(rt_ref, pos_ref, blk_ref, cnt_ref, run_ref, pst_ref, *, bm):
    ph = pl.program_id(0)
    c = pl.program_id(1)
    tp = rt_ref.shape[0]
    nbp = blk_ref.shape[0]
    lane = lax.broadcasted_iota(I32, (tp, LANES), 1).astype(F32)
    rt = rt_ref[...]
    sel0 = lane == rt[:, 0:1]
    sel1 = lane == rt[:, 1:2]
    both = sel0.astype(F32) + sel1.astype(F32)
    colsum = jnp.sum(both, axis=0, keepdims=True)

    @pl.when((ph == 0) & (c == 0))
    def _():
        cnt_ref[...] = jnp.zeros_like(cnt_ref)

    @pl.when(ph == 0)
    def _():
        cnt_ref[...] += colsum

    @pl.when((ph == 1) & (c == 0))
    def _():
        nblk = jnp.floor((cnt_ref[...] + (bm - 1)) * (1.0 / bm))
        r0 = lax.broadcasted_iota(I32, (LANES, LANES), 0)
        c0 = lax.broadcasted_iota(I32, (LANES, LANES), 1)
        upper = (r0 < c0).astype(BF16)
        nblk8 = jnp.broadcast_to(nblk, (8, LANES)).astype(BF16)
        pstb = jnp.dot(nblk8, upper, preferred_element_type=F32)[0:1, :]
        pst_ref[...] = pstb * bm
        run_ref[...] = jnp.zeros_like(run_ref)
        pend = pstb + nblk
        bidx = lax.broadcasted_iota(I32, (nbp, LANES), 0).astype(F32)
        l2 = lax.broadcasted_iota(I32, (nbp, LANES), 1)
        started = ((pend <= bidx) & (l2 < N_EXPERTS)).astype(F32)
        eid = jnp.minimum(jnp.sum(started, axis=1, keepdims=True), N_EXPERTS - 1.0)
        mine = l2.astype(F32) == eid
        cnt_e = jnp.sum(jnp.where(mine, cnt_ref[...], 0.0), axis=1, keepdims=True)
        first = jnp.sum(jnp.where(mine, pstb, 0.0), axis=1, keepdims=True)
        nvalid = jnp.clip(cnt_e - (bidx[:, 0:1] - first) * bm, 0.0, float(bm))
        blk_ref[...] = jnp.where(l2 == 0, eid, jnp.where(l2 == 1, nvalid, 0.0)).astype(I32)

    @pl.when(ph == 1)
    def _():
        r0 = lax.broadcasted_iota(I32, (tp, tp), 0)
        c0 = lax.broadcasted_iota(I32, (tp, tp), 1)
        lower = (c0 < r0).astype(BF16)
        before = jnp.dot(lower, both.astype(BF16), preferred_element_type=F32)
        base = before + run_ref[...] + pst_ref[...]
        d0 = jnp.sum(jnp.where(sel0, base, 0.0), axis=1, keepdims=True)
        d1 = jnp.sum(jnp.where(sel1, base, 0.0), axis=1, keepdims=True)
        pos_ref[...] = jnp.where(lane == 0.0, d0, jnp.where(lane == 1.0, d1, 0.0)).astype(I32)
        run_ref[...] += colsum


def _positions(route, nbp):
    n = route.shape[0]
    tp = TP_POS
    return pl.pallas_call(
        functools.partial(_pos_kernel, bm=BM_EXPERT),
        grid=(2, n // tp),
        in_specs=[pl.BlockSpec((tp, LANES), lambda ph, c: (c, 0))],
        out_specs=[
            pl.BlockSpec((tp, LANES), lambda ph, c: (c * ph, 0)),
            pl.BlockSpec((nbp, LANES), lambda ph, c: (0, 0)),
        ],
        out_shape=[
            jax.ShapeDtypeStruct((n, LANES), I32),
            jax.ShapeDtypeStruct((nbp, LANES), I32),
        ],
        scratch_shapes=[pltpu.VMEM((1, LANES), F32)] * 3,
        compiler_params=_cparams(("arbitrary", "arbitrary")),
        name="positions",
    )(route)


def _sc_mesh():
    return plsc.VectorSubcoreMesh(core_axis_name="c", subcore_axis_name="s",
                                  num_cores=SC_CORES, num_subcores=SC_SUBCORES)


def _sc_worker_base(rows_per_worker):
    return (lax.axis_index("s") * SC_CORES + lax.axis_index("c")) * rows_per_worker


def _dispatch(dest, xp, p_rows):
    n, half = xp.shape
    per_worker = n // SC_WORKERS
    chunk = SC_CHUNK
    assert n % (8 * SC_WORKERS) == 0 and per_worker % chunk == 0

    def body(xp_hbm, dest_hbm, xs_hbm, i0_v, i1_v, rows_v):
        base = _sc_worker_base(per_worker)

        @pl.loop(0, per_worker // chunk)
        def _(c):
            off = pl.multiple_of(base + c * chunk, 8)
            pltpu.sync_copy(dest_hbm.at[pl.ds(off, chunk)], i0_v)
            pltpu.sync_copy(dest_hbm.at[pl.ds(n + off, chunk)], i1_v)
            pltpu.sync_copy(xp_hbm.at[pl.ds(off, chunk)], rows_v)
            pltpu.sync_copy(rows_v, xs_hbm.at[i0_v])
            pltpu.sync_copy(rows_v, xs_hbm.at[i1_v])

    return pl.kernel(
        body, mesh=_sc_mesh(),
        out_type=jax.ShapeDtypeStruct((p_rows, half), xp.dtype),
        scratch_types=[pltpu.VMEM((chunk,), I32), pltpu.VMEM((chunk,), I32),
                       pltpu.VMEM((chunk, half), xp.dtype)],
        name="dispatch",
    )(xp, dest)


def _gather_rows(ys, dest):
    a_rows = dest.shape[0]
    d = ys.shape[1]
    per_worker = a_rows // SC_WORKERS
    chunk = SC_CHUNK
    assert a_rows % (8 * SC_WORKERS) == 0 and per_worker % chunk == 0

    def body(ys_hbm, dest_hbm, out_hbm, idx_v, rows_v):
        base = _sc_worker_base(per_worker)

        @pl.loop(0, per_worker // chunk)
        def _(c):
            off = pl.multiple_of(base + c * chunk, 8)
            pltpu.sync_copy(dest_hbm.at[pl.ds(off, chunk)], idx_v)
            pltpu.sync_copy(ys_hbm.at[idx_v], rows_v)
            pltpu.sync_copy(rows_v, out_hbm.at[pl.ds(off, chunk)])

    return pl.kernel(
        body, mesh=_sc_mesh(),
        out_type=jax.ShapeDtypeStruct((a_rows, d), ys.dtype),
        scratch_types=[pltpu.VMEM((chunk,), I32), pltpu.VMEM((chunk, d), ys.dtype)],
        name="gather_rows",
    )(ys, dest)


def _expert_kernel(eid_ref, nvalid_ref, xs_ref, wg_ref, wu_ref, wd_ref, ys_ref,
                   wgb_ref, wub_ref, wdb_ref):
    b = pl.program_id(0)
    nvalid = nvalid_ref[b]

    @pl.when((b == 0) | (eid_ref[b] != eid_ref[jnp.maximum(b - 1, 0)]))
    def _():
        wgb_ref[...] = wg_ref[...].astype(BF16)
        wub_ref[...] = wu_ref[...].astype(BF16)
        wdb_ref[...] = wd_ref[...].astype(BF16)

    @pl.when(nvalid != 0)
    def _():
        rows = lax.broadcasted_iota(I32, xs_ref.shape, 0)
        w = jnp.where(rows < nvalid, xs_ref[...], 0)
        half = w.shape[1]
        lo, hi = (v.astype(BF16) for v in _unpack_bf16_halves(w))

        def proj(w_ref):
            return (jnp.dot(lo, w_ref[:half, :], preferred_element_type=F32)
                    + jnp.dot(hi, w_ref[half:, :], preferred_element_type=F32))

        hg = proj(wgb_ref)
        hu = proj(wub_ref)
        act = (hg * jax.nn.sigmoid(hg) * hu).astype(BF16)
        y = jnp.dot(act, wdb_ref[...], preferred_element_type=F32)
        ys_ref[...] = _pack_bf16_halves(y.astype(BF16))

    @pl.when(nvalid == 0)
    def _():
        ys_ref[...] = jnp.zeros_like(ys_ref)


def _experts(eid, nvalid, xs, wg, wu, wd, layer):
    p, half = xs.shape
    d = 2 * half
    de = wg.shape[3]
    bm = BM_EXPERT
    return pl.pallas_call(
        _expert_kernel,
        grid_spec=pltpu.PrefetchScalarGridSpec(
            num_scalar_prefetch=2,
            grid=(p // bm,),
            in_specs=[
                pl.BlockSpec((bm, half), lambda b, e, v: (b, 0)),
                pl.BlockSpec((None, None, d, de), lambda b, e, v: (layer, e[b], 0, 0)),
                pl.BlockSpec((None, None, d, de), lambda b, e, v: (layer, e[b], 0, 0)),
                pl.BlockSpec((None, None, de, d), lambda b, e, v: (layer, e[b], 0, 0)),
            ],
            out_specs=pl.BlockSpec((bm, half), lambda b, e, v: (b, 0)),
            scratch_shapes=[pltpu.VMEM((d, de), BF16), pltpu.VMEM((d, de), BF16),
                            pltpu.VMEM((de, d), BF16)],
        ),
        out_shape=jax.ShapeDtypeStruct((p, half), I32),
        compiler_params=_cparams(("arbitrary",)),
        name="experts",
    )(eid, nvalid, xs, wg, wu, wd)


def _combine_kernel(h_ref, rt_ref, g0_ref, g1_ref, o_ref):
    rt = rt_ref[...]
    half = g0_ref.shape[1]
    lo0, hi0 = _unpack_bf16_halves(g0_ref[...])
    lo1, hi1 = _unpack_bf16_halves(g1_ref[...])
    w0, w1 = rt[:, 2:3], rt[:, 3:4]
    o_ref[:, :half] = h_ref[:, :half] + w0 * lo0 + w1 * lo1
    o_ref[:, half:] = h_ref[:, half:] + w0 * hi0 + w1 * hi1


def _combine(h, route, g):
    n, d = h.shape
    half = g.shape[1]
    tc = TC_COMBINE
    nt = n // tc
    return pl.pallas_call(
        _combine_kernel,
        grid=(nt,),
        in_specs=[pl.BlockSpec((tc, d), lambda i: (i, 0)),
                  pl.BlockSpec((tc, LANES), lambda i: (i, 0)),
                  pl.BlockSpec((tc, half), lambda i: (i, 0)),
                  pl.BlockSpec((tc, half), lambda i: (nt + i, 0))],
        out_specs=pl.BlockSpec((tc, d), lambda i: (i, 0)),
        out_shape=jax.ShapeDtypeStruct((n, d), F32),
        compiler_params=_cparams(("parallel",)),
        name="combine",
    )(h, route, g, g)


def _pad_lanes(v, width=LANES):
    return jnp.pad(v, [(0, 0)] * (v.ndim - 1) + [(0, width - v.shape[-1])])


def kernel(x, meta_tokens, norm1_g, w_in, b_forget, conv_w, w_conv_out, q_norm_g, k_norm_g,
           w_att_out, w_merge_out, norm2_g, w_router_group, b_router_group, w_router_expert,
           b_router_expert, w_exp_gate, w_exp_up, w_exp_down):
    batch, seq, d = x.shape
    depth = w_in.shape[0]
    length = seq + N_META
    lp = -(-length // SEQ_ALIGN) * SEQ_ALIGN
    n = batch * lp
    assert d == N_HEADS * HEAD_DIM == 1024
    assert lp % TQ_ATTN == 0 and lp % TM_MERGE == 0 and lp % TR_PREP == 0
    assert n % TM_INPROJ == 0 and n % TC_COMBINE == 0 and n % TP_POS == 0

    n_blocks = -(-(2 * n + N_EXPERTS * (BM_EXPERT - 1)) // BM_EXPERT)
    nbp = -(-n_blocks // 8) * 8
    p_rows = n_blocks * BM_EXPERT

    meta = jnp.broadcast_to(meta_tokens[None].astype(x.dtype), (batch, N_META, d))
    h = jnp.concatenate([meta, x, jnp.zeros((batch, lp - length, d), x.dtype)], axis=1)
    h = h.reshape(n, d)

    scale = HEAD_DIM ** -0.5
    for l in range(depth):
        w = w_in[l]
        w_all = jnp.concatenate([w[:, :6 * d], w[:, 6 * d + N_HEADS:]], axis=1).astype(BF16)
        w_f = _pad_lanes(w[:, 6 * d:6 * d + N_HEADS]).astype(BF16)
        proj, f = _inproj(h, norm1_g[l][None], w_all, w_f)

        bf = _pad_lanes(b_forget[l][None])
        gq2 = jnp.tile(q_norm_g[l], 2)[None] * (scale * LOG2E)
        gk2 = jnp.tile(k_norm_g[l], 2)[None]
        qa, ka, va = _prep(proj, f, bf, gq2, gk2, batch, lp)
        attn = _attention(qa, ka, va)

        w_r = _pad_lanes(jnp.concatenate([w_router_group[l], w_router_expert[l]], axis=1))
        wr_hi = w_r.astype(BF16)
        wr_lo = (w_r - wr_hi.astype(F32)).astype(BF16)
        b_r = _pad_lanes(jnp.concatenate([b_router_group[l], b_router_expert[l]])[None])
        cw = jnp.pad(conv_w[l], ((0, 8 - conv_w.shape[1]), (0, 0)))
        h_mid, xp, route = _merge(
            proj, attn, h, cw, w_conv_out[l].astype(BF16), w_att_out[l].astype(BF16),
            w_merge_out[l].astype(BF16), norm2_g[l][None], wr_hi, wr_lo, b_r, lp)

        pos, blk = _positions(route, nbp)
        dest = jnp.concatenate([pos[:, 0], pos[:, 1]])
        xs = _dispatch(dest, xp, p_rows)
        ys = _experts(blk[:n_blocks, 0], blk[:n_blocks, 1], xs,
                      w_exp_gate, w_exp_up, w_exp_down, l)
        h = _combine(h_mid, route, _gather_rows(ys, dest))

    return h.reshape(batch, lp, d)[:, N_META:length]
```

```python
import functools

import jax
import jax.numpy as jnp
from jax import lax
from jax.experimental import pallas as pl
from jax.experimental.pallas import tpu as pltpu
from jax.experimental.pallas import tpu_sc as plsc

F32 = jnp.float32
BF16 = jnp.bfloat16
I32 = jnp.int32

N_META = 16
N_HEADS = 16
HEAD_DIM = 64
N_GROUPS = 4
EXPERTS_PER_GROUP = 8
N_EXPERTS = N_GROUPS * EXPERTS_PER_GROUP
EPS = 1e-6
NEG = -1e30
LOG2E = 1.4426950408889634
BOUND_LANE = HEAD_DIM + 6
SAFE_BOUND_LOG2 = 50.0

LANES = 128
SEQ_ALIGN = 256
ROUTE_LANES = 4 + N_EXPERTS

TM_INPROJ = 1536
TN_INPROJ = 2048
TR_PREP = 768
TQ_ATTN = 768
TK_ATTN = 256
HEADS_PER_STEP = 4
TM_MERGE = 768
TP_POS = 1536
BM_EXPERT = 512
TC_COMBINE = 1408

SC_CORES = 2
SC_SUBCORES = 16
SC_WORKERS = SC_CORES * SC_SUBCORES
SC_CHUNK = 48

VMEM_LIMIT = 56 * 1024 * 1024


def _cparams(sem, vmem=VMEM_LIMIT):
    return pltpu.CompilerParams(dimension_semantics=sem, vmem_limit_bytes=vmem)


def _pack_bf16_halves(x):
    bits = lax.bitcast_convert_type(x.astype(F32), I32)
    half = x.shape[1] // 2
    return lax.shift_right_logical(bits[:, :half], 16) | (bits[:, half:] & jnp.int32(-65536))


def _unpack_bf16_halves(w):
    return (lax.bitcast_convert_type(w << 16, F32),
            lax.bitcast_convert_type(w & jnp.int32(-65536), F32))


def _inproj_kernel(x_ref, g_ref, w_ref, wf_ref, o_ref, f_ref, hn_ref):
    @pl.when(pl.program_id(1) == 0)
    def _():
        x = x_ref[...]
        r = lax.rsqrt(jnp.mean(x * x, axis=-1, keepdims=True) + EPS)
        hn = (x * r * g_ref[...]).astype(BF16)
        hn_ref[...] = hn
        f_ref[...] = jnp.dot(hn, wf_ref[...], preferred_element_type=F32)

    o_ref[...] = jnp.dot(hn_ref[...], w_ref[...], preferred_element_type=F32).astype(BF16)


def _inproj(h, g, w_all, w_f):
    n, d = h.shape
    cols = w_all.shape[1]
    tm, tn = TM_INPROJ, TN_INPROJ
    return pl.pallas_call(
        _inproj_kernel,
        grid=(n // tm, cols // tn),
        in_specs=[
            pl.BlockSpec((tm, d), lambda i, j: (i, 0)),
            pl.BlockSpec((1, d), lambda i, j: (0, 0)),
            pl.BlockSpec((d, tn), lambda i, j: (0, j)),
            pl.BlockSpec((d, LANES), lambda i, j: (0, 0)),
        ],
        out_specs=[
            pl.BlockSpec((tm, tn), lambda i, j: (i, j)),
            pl.BlockSpec((tm, LANES), lambda i, j: (i, 0)),
        ],
        out_shape=[
            jax.ShapeDtypeStruct((n, cols), BF16),
            jax.ShapeDtypeStruct((n, LANES), F32),
        ],
        scratch_shapes=[pltpu.VMEM((tm, d), BF16)],
        compiler_params=_cparams(("parallel", "arbitrary")),
        name="inproj",
    )(h, g, w_all, w_f)


def _split3(c):
    hi = c.astype(BF16).astype(F32)
    r1 = c - hi
    mid = r1.astype(BF16).astype(F32)
    lo = (r1 - mid).astype(BF16).astype(F32)
    return hi, mid, lo


def _prep_kernel(q_ref, k_ref, v_ref, f_ref, bf_ref, gq_ref, gk_ref, hsel_ref, hexp_ref,
                 selq_ref, selk_ref,
                 qa_ref, ka_ref, va_ref, carry_ref):
    tr = f_ref.shape[0]

    @pl.when(pl.program_id(1) == 0)
    def _():
        carry_ref[...] = jnp.zeros_like(carry_ref)

    f = f_ref[...] + bf_ref[...]
    lf = jnp.minimum(f, 0.0) - jnp.log1p(jnp.exp(-jnp.abs(f)))
    row = lax.broadcasted_iota(I32, (tr, tr), 0)
    col = lax.broadcasted_iota(I32, (tr, tr), 1)
    tril = (col <= row).astype(F32)
    c = jnp.dot(tril, lf, precision=lax.Precision.HIGHEST,
                preferred_element_type=F32) + carry_ref[...]
    carry_ref[...] = c[tr - 1:tr, :]
    c_hi, c_mid, c_lo = _split3(c * LOG2E)

    lane = lax.broadcasted_iota(I32, (tr, LANES), 1)
    low = lane < HEAD_DIM
    ones_v = (lane == HEAD_DIM).astype(F32)

    heads = lane < N_HEADS
    c3 = (jnp.where(heads, c_hi, 0.0)
          + pltpu.roll(jnp.where(heads, c_mid, 0.0), N_HEADS, 1)
          + pltpu.roll(jnp.where(heads, c_lo, 0.0), 2 * N_HEADS, 1)
          + (lane == LANES - 1).astype(F32)).astype(BF16)

    def inv_rms(x):
        ss = jnp.dot((x * x).astype(BF16), hsel_ref[...], preferred_element_type=F32)
        r = lax.rsqrt(ss * (1.0 / HEAD_DIM) + EPS)
        r_hi = r.astype(BF16)
        r_lo = (r - r_hi.astype(F32)).astype(BF16)
        return jnp.dot(jnp.concatenate([r_hi, r_lo], axis=1), hexp_ref[...],
                       preferred_element_type=F32)

    rq = inv_rms(q_ref[...].astype(F32))
    rk = inv_rms(k_ref[...].astype(F32))

    for p in range(N_HEADS // 2):
        sl = slice(p * LANES, (p + 1) * LANES)
        e, o = 2 * p, 2 * p + 1
        qn = q_ref[:, sl].astype(F32) * rq[:, sl] * gq_ref[...]
        kn = k_ref[:, sl].astype(F32) * rk[:, sl] * gk_ref[...]
        vv = v_ref[:, sl].astype(F32)
        pc = slice(2 * p * LANES, (2 * p + 2) * LANES)
        ext_q = jnp.dot(c3, selq_ref[:, pc], preferred_element_type=F32)
        ext_k = jnp.dot(c3, selk_ref[:, pc], preferred_element_type=F32)
        for h in (e, o):
            if h == e:
                qh, kh, vh = qn, kn, vv
            else:
                qh = pltpu.roll(qn, HEAD_DIM, 1)
                kh = pltpu.roll(kn, HEAD_DIM, 1)
                vh = pltpu.roll(vv, HEAD_DIM, 1)
            hs = slice((h - e) * LANES, (h - e + 1) * LANES)
            qa_ref[0, h] = jnp.where(low, qh, ext_q[:, hs]).astype(BF16)
            ka_ref[0, h] = jnp.where(low, kh, ext_k[:, hs]).astype(BF16)
            va_ref[0, h] = jnp.where(low, vh, ones_v).astype(BF16)


def _prep_constants():
    feat = jnp.arange(N_HEADS * HEAD_DIM)
    hsel = (feat[:, None] // HEAD_DIM == jnp.arange(LANES)[None, :]).astype(BF16)
    hexp = jnp.concatenate([hsel.T, hsel.T], axis=0)
    rows = jnp.arange(LANES)
    part, src = rows // N_HEADS, rows % N_HEADS
    cols = jnp.arange(N_HEADS * LANES)
    head, off = cols // LANES, cols % LANES - HEAD_DIM
    is_head = (src[:, None] == head[None, :]) & (part[:, None] < 3)
    const_row = (rows == LANES - 1)[:, None]
    sel_q = (is_head & (off[None, :] == part[:, None])).astype(F32) \
        + (const_row & (off[None, :] >= 3) & (off[None, :] < 6)).astype(F32)
    sel_k = -(is_head & (off[None, :] == part[:, None] + 3)).astype(F32) \
        + (const_row & (((off[None, :] >= 0) & (off[None, :] < 3))
                        | (off[None, :] == BOUND_LANE - HEAD_DIM))).astype(F32)
    return hsel, hexp, sel_q.astype(BF16), sel_k.astype(BF16)


def _prep(proj, f, bf, gq2, gk2, batch, lp):
    tr = TR_PREP
    nr = lp // tr
    hsel, hexp, sel_q, sel_k = _prep_constants()
    aug = jax.ShapeDtypeStruct((batch, N_HEADS, lp, LANES), BF16)
    aug_spec = pl.BlockSpec((1, N_HEADS, tr, LANES), lambda b, r: (b, 0, r, 0))
    vec = pl.BlockSpec((1, LANES), lambda b, r: (0, 0))
    full = lambda a: pl.BlockSpec(a.shape, lambda b, r: (0, 0))
    return pl.pallas_call(
        _prep_kernel,
        grid=(batch, nr),
        in_specs=[
            pl.BlockSpec((tr, 1024), lambda b, r: (b * nr + r, 3)),
            pl.BlockSpec((tr, 1024), lambda b, r: (b * nr + r, 4)),
            pl.BlockSpec((tr, 1024), lambda b, r: (b * nr + r, 5)),
            pl.BlockSpec((tr, LANES), lambda b, r: (b * nr + r, 0)),
            vec, vec, vec, full(hsel), full(hexp), full(sel_q), full(sel_k),
        ],
        out_specs=[aug_spec, aug_spec, aug_spec],
        out_shape=[aug, aug, aug],
        scratch_shapes=[pltpu.VMEM((1, LANES), F32)],
        compiler_params=_cparams(("parallel", "arbitrary")),
        name="prep",
    )(proj, proj, proj, f, bf, gq2, gk2, hsel, hexp, sel_q, sel_k)


def _attn_kernel(qa_ref, ka_ref, va_ref, o_ref, acc_ref, kmax_ref, s_ref, *, tq, tk):
    i = pl.program_id(2)
    nh = qa_ref.shape[1]
    lp = ka_ref.shape[2]
    assert tq % tk == 0 and nh % 2 == 0
    n_diag = tq // tk
    n_full = i * n_diag
    lane = lax.broadcasted_iota(I32, (tq, LANES), 1)
    low = lane < HEAD_DIM
    nt = (((1,), (1,)), ((), ()))

    @pl.when(i == 0)
    def _():
        feat = (lax.broadcasted_iota(I32, (LANES, LANES), 0) < HEAD_DIM).astype(BF16)

        def chunk(r, best):
            out = []
            for hh in range(nh):
                kk = ka_ref[0, hh, pl.ds(pl.multiple_of(r * tk, tk), tk), :]
                n2 = jnp.dot(kk * kk, feat, preferred_element_type=F32)
                out.append(jnp.maximum(best[hh], jnp.max(n2.reshape(tk // 8, 8, LANES), axis=0)))
            return tuple(out)

        best = lax.fori_loop(0, lp // tk, chunk, (jnp.zeros((8, LANES), F32),) * nh)
        for hh in range(nh):
            kmax_ref[hh] = jnp.broadcast_to(1.01 * jnp.max(best[hh], axis=0, keepdims=True),
                                            (8, LANES))

    def kv(hh, j):
        start = pl.multiple_of(j * tk, tk)
        return (ka_ref[0, hh, pl.ds(start, tk), :], va_ref[0, hh, pl.ds(start, tk), :])

    qb = []
    worst = jnp.zeros((1, 1), F32)
    feat_q = (lax.broadcasted_iota(I32, (LANES, LANES), 0) < HEAD_DIM).astype(BF16)
    for hh in range(nh):
        q = qa_ref[0, hh]
        qn2 = 1.01 * jnp.dot(q * q, feat_q, preferred_element_type=F32)
        bound = jnp.sqrt(qn2 * kmax_ref[hh][0:1, :]) + 0.25
        qb.append(jnp.where(lane == BOUND_LANE, -bound, q.astype(F32)).astype(BF16))
        worst = jnp.maximum(worst, jnp.max(bound, axis=0, keepdims=True)[:, 0:1])
    safe = worst[0, 0] <= SAFE_BOUND_LOG2

    @pl.when(safe)
    def _():
        acc_ref[...] = jnp.zeros_like(acc_ref)

        skew = (lax.broadcasted_iota(I32, (tq, tk), 1) - lax.broadcasted_iota(I32, (tq, tk), 0))
        last = lp // tk - 1

        def logits(j, buf, r0=0):
            jj = jnp.minimum(j, last)
            for hh in range(nh):
                k, _ = kv(hh, jj)
                s = lax.dot_general(qb[hh][r0:], k, nt, preferred_element_type=F32)
                s_ref[buf, hh, r0:, :] = jnp.where(skew[r0:] <= i * tq - j * tk, s, NEG)

        def consume(j, buf, r0=0):
            jj = jnp.minimum(j, last)
            for hh in range(nh):
                _, v = kv(hh, jj)
                p = jnp.exp2(s_ref[buf, hh, r0:, :]).astype(BF16)
                acc_ref[hh, r0:, :] += jnp.dot(p, v, preferred_element_type=F32)

        def step(j, cur, nxt):
            logits(j + 1, nxt)
            consume(j, cur)

        def quad(r, carry):
            for u in range(4):
                step(4 * r + u, u % 2, 1 - u % 2)
            return carry

        n_u = n_full + 1
        logits(0, 0)
        quads = (n_u - 1) // 4
        lax.fori_loop(0, quads, quad, 0)
        j4 = 4 * quads
        pairs = (n_u - j4) // 2

        def pair(r, carry):
            for u in range(2):
                step(j4 + 2 * r + u, u, 1 - u)
            return carry

        lax.fori_loop(0, pairs, pair, 0)
        tail = j4 + 2 * pairs

        def finish(first_d):
            pending = lambda: consume(tail, 0)
            buf = 1
            for dd in range(first_d, n_diag):
                logits(n_full + dd, buf, dd * tk)
                pending()
                pending = lambda dd=dd, buf=buf: consume(n_full + dd, buf, dd * tk)
                buf = 1 - buf
            pending()

        @pl.when(tail == n_full)
        def _():
            finish(1)

        @pl.when(tail != n_full)
        def _():
            finish(2)

    @pl.when(jnp.logical_not(safe))
    def _():
        for hh in range(nh):
            q = qa_ref[0, hh]

            def tile(j, carry, masked, hh=hh, q=q):
                m, acc = carry
                k, v = kv(hh, j)
                s = lax.dot_general(q, k, nt, preferred_element_type=F32)
                if masked:
                    qpos = i * tq + lax.broadcasted_iota(I32, (tq, tk), 0)
                    kpos = j * tk + lax.broadcasted_iota(I32, (tq, tk), 1)
                    s = jnp.where(kpos <= qpos, s, NEG)
                m_new = jnp.maximum(m, jnp.max(s, axis=1, keepdims=True))
                acc = jnp.exp2(m - m_new) * acc + jnp.dot(
                    jnp.exp2(s - m_new).astype(BF16), v, preferred_element_type=F32)
                return m_new, acc

            carry = (jnp.full((tq, 1), NEG, F32), jnp.zeros((tq, LANES), F32))
            carry = lax.fori_loop(0, n_full, functools.partial(tile, masked=False), carry)
            for d in range(n_diag):
                carry = tile(n_full + d, carry, True)
            acc_ref[hh] = carry[1]

    for pr in range(nh // 2):
        a0 = acc_ref[2 * pr]
        a1 = acc_ref[2 * pr + 1]
        o0 = a0 / a0[:, HEAD_DIM:HEAD_DIM + 1]
        o1 = a1 / a1[:, HEAD_DIM:HEAD_DIM + 1]
        o_ref[:, pr * LANES:(pr + 1) * LANES] = jnp.where(
            low, o0, pltpu.roll(o1, HEAD_DIM, 1)).astype(BF16)


def _attention(qa, ka, va):
    batch, _, lp, _ = qa.shape
    tq, tk, nh = TQ_ATTN, TK_ATTN, HEADS_PER_STEP
    nq = lp // tq
    return pl.pallas_call(
        functools.partial(_attn_kernel, tq=tq, tk=tk),
        grid=(batch, N_HEADS // nh, nq),
        in_specs=[
            pl.BlockSpec((1, nh, tq, LANES), lambda b, h, i: (b, h, i, 0)),
            pl.BlockSpec((1, nh, lp, LANES), lambda b, h, i: (b, h, 0, 0)),
            pl.BlockSpec((1, nh, lp, LANES), lambda b, h, i: (b, h, 0, 0)),
        ],
        out_specs=pl.BlockSpec((tq, nh * HEAD_DIM), lambda b, h, i: (b * nq + i, h)),
        out_shape=jax.ShapeDtypeStruct((batch * lp, N_HEADS * HEAD_DIM), BF16),
        scratch_shapes=[pltpu.VMEM((nh, tq, LANES), F32), pltpu.VMEM((nh, 8, LANES), F32),
                        pltpu.VMEM((2, nh, tq, tk), F32)],
        compiler_params=_cparams(("parallel", "parallel", "arbitrary")),
        name="attn",
    )(qa, ka, va)


def _merge_kernel(b_ref, c_ref, u_ref, ch_ref, uh_ref, ga_ref, gb_ref, at_ref, h_ref,
                  cw_ref, wc_ref, wa_ref, wm_ref, g2_ref, wrh_ref, wrl_ref, br_ref,
                  ho_ref, xp_ref, rt_ref, *, tiles_per_seq):
    tm, d = h_ref.shape
    i = pl.program_id(0)

    z = c_ref[...].astype(F32) * u_ref[...].astype(F32)
    hv = (i % tiles_per_seq != 0).astype(F32)
    zh = ch_ref[...].astype(F32) * uh_ref[...].astype(F32) * hv
    hr = zh.shape[0]
    zm1 = zh[hr - 1:hr, :]
    zm2 = zh[hr - 2:hr - 1, :]
    rowid = lax.broadcasted_iota(I32, (tm, d), 0)
    zp1 = jnp.where(rowid == 0, zm1, pltpu.roll(z, 1, 0))
    zp2 = jnp.where(rowid == 0, zm2, jnp.where(rowid == 1, zm1, pltpu.roll(z, 2, 0)))
    cw = cw_ref[...]
    yc = b_ref[...].astype(F32) * (zp2 * cw[0:1, :] + zp1 * cw[1:2, :] + z * cw[2:3, :])

    y_a = jnp.dot(yc.astype(BF16), wc_ref[...], preferred_element_type=F32)
    y_b = jnp.dot(at_ref[...], wa_ref[...], preferred_element_type=F32)
    mix = (jax.nn.sigmoid(ga_ref[...].astype(F32)) * y_a
           + jax.nn.sigmoid(gb_ref[...].astype(F32)) * y_b)
    hnew = h_ref[...] + jnp.dot(mix.astype(BF16), wm_ref[...], preferred_element_type=F32)
    ho_ref[...] = hnew

    r = lax.rsqrt(jnp.mean(hnew * hnew, axis=-1, keepdims=True) + EPS)
    xn = hnew * r * g2_ref[...]
    x_hi = xn.astype(BF16)
    xp_ref[...] = _pack_bf16_halves(x_hi)

    x_lo = (xn - x_hi.astype(F32)).astype(BF16)
    logits = (jnp.dot(x_hi, wrh_ref[...], preferred_element_type=F32)
              + jnp.dot(x_lo, wrh_ref[...], preferred_element_type=F32)
              + jnp.dot(x_hi, wrl_ref[...], preferred_element_type=F32)) + br_ref[...]

    lanei = lax.broadcasted_iota(I32, (tm, LANES), 1)
    lane = lanei.astype(F32)
    big = jnp.float32(1e9)
    is_g = lanei < N_GROUPS
    is_e = (lanei >= N_GROUPS) & (lanei < ROUTE_LANES)
    gl = jnp.where(is_g, logits, NEG)
    gmax = jnp.max(gl, axis=1, keepdims=True)
    gidx = jnp.min(jnp.where(gl == gmax, lane, big), axis=1, keepdims=True)
    gsum = jnp.sum(jnp.where(is_g, jnp.exp(gl - gmax), 0.0), axis=1, keepdims=True)
    grp = ((lanei - N_GROUPS) >> 3).astype(F32)
    el = jnp.where(is_e & (grp == gidx), logits, NEG)
    emax = jnp.max(el, axis=1, keepdims=True)
    i1 = jnp.min(jnp.where(el == emax, lane, big), axis=1, keepdims=True)
    el2 = jnp.where(lane == i1, NEG, el)
    e2max = jnp.max(el2, axis=1, keepdims=True)
    i2 = jnp.min(jnp.where(el2 == e2max, lane, big), axis=1, keepdims=True)
    t = jnp.exp(e2max - emax)
    w1 = 1.0 / (gsum * (1.0 + t))
    w2 = w1 * t
    rt_ref[...] = jnp.where(lanei == 0, i1 - N_GROUPS,
                            jnp.where(lanei == 1, i2 - N_GROUPS,
                                      jnp.where(lanei == 2, w1,
                                                jnp.where(lanei == 3, w2, 0.0))))


def _merge(proj, attn, h, cw, wc, wa, wm, g2, wr_hi, wr_lo, br, lp):
    n, d = h.shape
    tm = TM_MERGE
    halo = 16
    hb = tm // halo
    row = lambda c: pl.BlockSpec((tm, d), lambda i: (i, c))
    halo_spec = lambda c: pl.BlockSpec((halo, d), lambda i: (jnp.maximum(i * hb - 1, 0), c))
    full = lambda r, c: pl.BlockSpec((r, c), lambda i: (0, 0))
    return pl.pallas_call(
        functools.partial(_merge_kernel, tiles_per_seq=lp // tm),
        grid=(n // tm,),
        in_specs=[
            row(0), row(1), row(2), halo_spec(1), halo_spec(2), row(6), row(7),
            pl.BlockSpec((tm, d), lambda i: (i, 0)),
            pl.BlockSpec((tm, d), lambda i: (i, 0)),
            full(8, d), full(d, d), full(d, d), full(d, d), full(1, d),
            full(d, LANES), full(d, LANES), full(1, LANES),
        ],
        out_specs=[
            pl.BlockSpec((tm, d), lambda i: (i, 0)),
            pl.BlockSpec((tm, d // 2), lambda i: (i, 0)),
            pl.BlockSpec((tm, LANES), lambda i: (i, 0)),
        ],
        out_shape=[
            jax.ShapeDtypeStruct((n, d), F32),
            jax.ShapeDtypeStruct((n, d // 2), I32),
            jax.ShapeDtypeStruct((n, LANES), F32),
        ],
        compiler_params=_cparams(("parallel",)),
        name="merge",
    )(proj, proj, proj, proj, proj, proj, proj, attn, h,
      cw, wc, wa, wm, g2, wr_hi, wr_lo, br)


def _pos_kernel(rt_ref, pos_ref, blk_ref, cnt_ref, run_ref, pst_ref, *, bm):
    ph = pl.program_id(0)
    c = pl.program_id(1)
    tp = rt_ref.shape[0]
    nbp = blk_ref.shape[0]
    lane = lax.broadcasted_iota(I32, (tp, LANES), 1).astype(F32)
    rt = rt_ref[...]
    sel0 = lane == rt[:, 0:1]
    sel1 = lane == rt[:, 1:2]
    both = sel0.astype(F32) + sel1.astype(F32)
    colsum = jnp.sum(both, axis=0, keepdims=True)

    @pl.when((ph == 0) & (c == 0))
    def _():
        cnt_ref[...] = jnp.zeros_like(cnt_ref)

    @pl.when(ph == 0)
    def _():
        cnt_ref[...] += colsum

    @pl.when((ph == 1) & (c == 0))
    def _():
        nblk = jnp.floor((cnt_ref[...] + (bm - 1)) * (1.0 / bm))
        r0 = lax.broadcasted_iota(I32, (LANES, LANES), 0)
        c0 = lax.broadcasted_iota(I32, (LANES, LANES), 1)
        upper = (r0 < c0).astype(BF16)
        nblk8 = jnp.broadcast_to(nblk, (8, LANES)).astype(BF16)
        pstb = jnp.dot(nblk8, upper, preferred_element_type=F32)[0:1, :]
        pst_ref[...] = pstb * bm
        run_ref[...] = jnp.zeros_like(run_ref)
        pend = pstb + nblk
        bidx = lax.broadcasted_iota(I32, (nbp, LANES), 0).astype(F32)
        l2 = lax.broadcasted_iota(I32, (nbp, LANES), 1)
        started = ((pend <= bidx) & (l2 < N_EXPERTS)).astype(F32)
        eid = jnp.minimum(jnp.sum(started, axis=1, keepdims=True), N_EXPERTS - 1.0)
        mine = l2.astype(F32) == eid
        cnt_e = jnp.sum(jnp.where(mine, cnt_ref[...], 0.0), axis=1, keepdims=True)
        first = jnp.sum(jnp.where(mine, pstb, 0.0), axis=1, keepdims=True)
        nvalid = jnp.clip(cnt_e - (bidx[:, 0:1] - first) * bm, 0.0, float(bm))
        blk_ref[...] = jnp.where(l2 == 0, eid, jnp.where(l2 == 1, nvalid, 0.0)).astype(I32)

    @pl.when(ph == 1)
    def _():
        r0 = lax.broadcasted_iota(I32, (tp, tp), 0)
        c0 = lax.broadcasted_iota(I32, (tp, tp), 1)
        lower = (c0 < r0).astype(BF16)
        before = jnp.dot(lower, both.astype(BF16), preferred_element_type=F32)
        base = before + run_ref[...] + pst_ref[...]
        d0 = jnp.sum(jnp.where(sel0, base, 0.0), axis=1, keepdims=True)
        d1 = jnp.sum(jnp.where(sel1, base, 0.0), axis=1, keepdims=True)
        pos_ref[...] = jnp.where(lane == 0.0, d0, jnp.where(lane == 1.0, d1, 0.0)).astype(I32)
        run_ref[...] += colsum


def _positions(route, nbp):
    n = route.shape[0]
    tp = TP_POS
    return pl.pallas_call(
        functools.partial(_pos_kernel, bm=BM_EXPERT),
        grid=(2, n // tp),
        in_specs=[pl.BlockSpec((tp, LANES), lambda ph, c: (c, 0))],
        out_specs=[
            pl.BlockSpec((tp, LANES), lambda ph, c: (c * ph, 0)),
            pl.BlockSpec((nbp, LANES), lambda ph, c: (0, 0)),
        ],
        out_shape=[
            jax.ShapeDtypeStruct((n, LANES), I32),
            jax.ShapeDtypeStruct((nbp, LANES), I32),
        ],
        scratch_shapes=[pltpu.VMEM((1, LANES), F32)] * 3,
        compiler_params=_cparams(("arbitrary", "arbitrary")),
        name="positions",
    )(route)


def _sc_mesh():
    return plsc.VectorSubcoreMesh(core_axis_name="c", subcore_axis_name="s",
                                  num_cores=SC_CORES, num_subcores=SC_SUBCORES)


def _sc_worker_base(rows_per_worker):
    return (lax.axis_index("s") * SC_CORES + lax.axis_index("c")) * rows_per_worker


def _dispatch(dest, xp, p_rows):
    n, half = xp.shape
    per_worker = n // SC_WORKERS
    chunk = SC_CHUNK
    assert n % (8 * SC_WORKERS) == 0 and per_worker % chunk == 0

    def body(xp_hbm, dest_hbm, xs_hbm, i0_v, i1_v, rows_v):
        base = _sc_worker_base(per_worker)

        @pl.loop(0, per_worker // chunk)
        def _(c):
            off = pl.multiple_of(base + c * chunk, 8)
            pltpu.sync_copy(dest_hbm.at[pl.ds(off, chunk)], i0_v)
            pltpu.sync_copy(dest_hbm.at[pl.ds(n + off, chunk)], i1_v)
            pltpu.sync_copy(xp_hbm.at[pl.ds(off, chunk)], rows_v)
            pltpu.sync_copy(rows_v, xs_hbm.at[i0_v])
            pltpu.sync_copy(rows_v, xs_hbm.at[i1_v])

    return pl.kernel(
        body, mesh=_sc_mesh(),
        out_type=jax.ShapeDtypeStruct((p_rows, half), xp.dtype),
        scratch_types=[pltpu.VMEM((chunk,), I32), pltpu.VMEM((chunk,), I32),
                       pltpu.VMEM((chunk, half), xp.dtype)],
        name="dispatch",
    )(xp, dest)


def _gather_rows(ys, dest):
    a_rows = dest.shape[0]
    d = ys.shape[1]
    per_worker = a_rows // SC_WORKERS
    chunk = SC_CHUNK
    assert a_rows % (8 * SC_WORKERS) == 0 and per_worker % chunk == 0

    def body(ys_hbm, dest_hbm, out_hbm, idx_v, rows_v):
        base = _sc_worker_base(per_worker)

        @pl.loop(0, per_worker // chunk)
        def _(c):
            off = pl.multiple_of(base + c * chunk, 8)
            pltpu.sync_copy(dest_hbm.at[pl.ds(off, chunk)], idx_v)
            pltpu.sync_copy(ys_hbm.at[idx_v], rows_v)
            pltpu.sync_copy(rows_v, out_hbm.at[pl.ds(off, chunk)])

    return pl.kernel(
        body, mesh=_sc_mesh(),
        out_type=jax.ShapeDtypeStruct((a_rows, d), ys.dtype),
        scratch_types=[pltpu.VMEM((chunk,), I32), pltpu.VMEM((chunk, d), ys.dtype)],
        name="gather_rows",
    )(ys, dest)


def _expert_kernel(eid_ref, nvalid_ref, xs_ref, wg_ref, wu_ref, wd_ref, ys_ref,
                   wgb_ref, wub_ref, wdb_ref):
    b = pl.program_id(0)
    nvalid = nvalid_ref[b]

    @pl.when((b == 0) | (eid_ref[b] != eid_ref[jnp.maximum(b - 1, 0)]))
    def _():
        wgb_ref[...] = wg_ref[...].astype(BF16)
        wub_ref[...] = wu_ref[...].astype(BF16)
        wdb_ref[...] = wd_ref[...].astype(BF16)

    @pl.when(nvalid != 0)
    def _():
        rows = lax.broadcasted_iota(I32, xs_ref.shape, 0)
        w = jnp.where(rows < nvalid, xs_ref[...], 0)
        half = w.shape[1]
        lo, hi = (v.astype(BF16) for v in _unpack_bf16_halves(w))

        def proj(w_ref):
            return (jnp.dot(lo, w_ref[:half, :], preferred_element_type=F32)
                    + jnp.dot(hi, w_ref[half:, :], preferred_element_type=F32))

        hg = proj(wgb_ref)
        hu = proj(wub_ref)
        act = (hg * jax.nn.sigmoid(hg) * hu).astype(BF16)
        y = jnp.dot(act, wdb_ref[...], preferred_element_type=F32)
        ys_ref[...] = _pack_bf16_halves(y.astype(BF16))

    @pl.when(nvalid == 0)
    def _():
        ys_ref[...] = jnp.zeros_like(ys_ref)


def _experts(eid, nvalid, xs, wg, wu, wd, layer):
    p, half = xs.shape
    d = 2 * half
    de = wg.shape[3]
    bm = BM_EXPERT
    return pl.pallas_call(
        _expert_kernel,
        grid_spec=pltpu.PrefetchScalarGridSpec(
            num_scalar_prefetch=2,
            grid=(p // bm,),
            in_specs=[
                pl.BlockSpec((bm, half), lambda b, e, v: (b, 0)),
                pl.BlockSpec((None, None, d, de), lambda b, e, v: (layer, e[b], 0, 0)),
                pl.BlockSpec((None, None, d, de), lambda b, e, v: (layer, e[b], 0, 0)),
                pl.BlockSpec((None, None, de, d), lambda b, e, v: (layer, e[b], 0, 0)),
            ],
            out_specs=pl.BlockSpec((bm, half), lambda b, e, v: (b, 0)),
            scratch_shapes=[pltpu.VMEM((d, de), BF16), pltpu.VMEM((d, de), BF16),
                            pltpu.VMEM((de, d), BF16)],
        ),
        out_shape=jax.ShapeDtypeStruct((p, half), I32),
        compiler_params=_cparams(("arbitrary",)),
        name="experts",
    )(eid, nvalid, xs, wg, wu, wd)


def _combine_kernel(h_ref, rt_ref, g0_ref, g1_ref, o_ref):
    rt = rt_ref[...]
    half = g0_ref.shape[1]
    lo0, hi0 = _unpack_bf16_halves(g0_ref[...])
    lo1, hi1 = _unpack_bf16_halves(g1_ref[...])
    w0, w1 = rt[:, 2:3], rt[:, 3:4]
    o_ref[:, :half] = h_ref[:, :half] + w0 * lo0 + w1 * lo1
    o_ref[:, half:] = h_ref[:, half:] + w0 * hi0 + w1 * hi1


def _combine(h, route, g):
    n, d = h.shape
    half = g.shape[1]
    tc = TC_COMBINE
    nt = n // tc
    return pl.pallas_call(
        _combine_kernel,
        grid=(nt,),
        in_specs=[pl.BlockSpec((tc, d), lambda i: (i, 0)),
                  pl.BlockSpec((tc, LANES), lambda i: (i, 0)),
                  pl.BlockSpec((tc, half), lambda i: (i, 0)),
                  pl.BlockSpec((tc, half), lambda i: (nt + i, 0))],
        out_specs=pl.BlockSpec((tc, d), lambda i: (i, 0)),
        out_shape=jax.ShapeDtypeStruct((n, d), F32),
        compiler_params=_cparams(("parallel",)),
        name="combine",
    )(h, route, g, g)


def _pad_lanes(v, width=LANES):
    return jnp.pad(v, [(0, 0)] * (v.ndim - 1) + [(0, width - v.shape[-1])])


def kernel(x, meta_tokens, norm1_g, w_in, b_forget, conv_w, w_conv_out, q_norm_g, k_norm_g,
           w_att_out, w_merge_out, norm2_g, w_router_group, b_router_group, w_router_expert,
           b_router_expert, w_exp_gate, w_exp_up, w_exp_down):
    batch, seq, d = x.shape
    depth = w_in.shape[0]
    length = seq + N_META
    lp = -(-length // SEQ_ALIGN) * SEQ_ALIGN
    n = batch * lp
    assert d == N_HEADS * HEAD_DIM == 1024
    assert lp % TQ_ATTN == 0 and lp % TM_MERGE == 0 and lp % TR_PREP == 0
    assert n % TM_INPROJ == 0 and n % TC_COMBINE == 0 and n % TP_POS == 0

    n_blocks = -(-(2 * n + N_EXPERTS * (BM_EXPERT - 1)) // BM_EXPERT)
    nbp = -(-n_blocks // 8) * 8
    p_rows = n_blocks * BM_EXPERT

    meta = jnp.broadcast_to(meta_tokens[None].astype(x.dtype), (batch, N_META, d))
    h = jnp.concatenate([meta, x, jnp.zeros((batch, lp - length, d), x.dtype)], axis=1)
    h = h.reshape(n, d)

    scale = HEAD_DIM ** -0.5
    for l in range(depth):
        w = w_in[l]
        w_all = jnp.concatenate([w[:, :6 * d], w[:, 6 * d + N_HEADS:]], axis=1).astype(BF16)
        w_f = _pad_lanes(w[:, 6 * d:6 * d + N_HEADS]).astype(BF16)
        proj, f = _inproj(h, norm1_g[l][None], w_all, w_f)

        bf = _pad_lanes(b_forget[l][None])
        gq2 = jnp.tile(q_norm_g[l], 2)[None] * (scale * LOG2E)
        gk2 = jnp.tile(k_norm_g[l], 2)[None]
        qa, ka, va = _prep(proj, f, bf, gq2, gk2, batch, lp)
        attn = _attention(qa, ka, va)

        w_r = _pad_lanes(jnp.concatenate([w_router_group[l], w_router_expert[l]], axis=1))
        wr_hi = w_r.astype(BF16)
        wr_lo = (w_r - wr_hi.astype(F32)).astype(BF16)
        b_r = _pad_lanes(jnp.concatenate([b_router_group[l], b_router_expert[l]])[None])
        cw = jnp.pad(conv_w[l], ((0, 8 - conv_w.shape[1]), (0, 0)))
        h_mid, xp, route = _merge(
            proj, attn, h, cw, w_conv_out[l].astype(BF16), w_att_out[l].astype(BF16),
            w_merge_out[l].astype(BF16), norm2_g[l][None], wr_hi, wr_lo, b_r, lp)

        pos, blk = _positions(route, nbp)
        dest = jnp.concatenate([pos[:, 0], pos[:, 1]])
        xs = _dispatch(dest, xp, p_rows)
        ys = _experts(blk[:n_blocks, 0], blk[:n_blocks, 1], xs,
                      w_exp_gate, w_exp_up, w_exp_down, l)
        h = _combine(h_mid, route, _gather_rows(ys, dest))

    return h.reshape(batch, lp, d)[:, N_META:length]
```

```python
import functools

import jax
import jax.numpy as jnp
from jax import lax
from jax.experimental import pallas as pl
from jax.experimental.pallas import tpu as pltpu
from jax.experimental.pallas import tpu_sc as plsc

F32 = jnp.float32
BF16 = jnp.bfloat16
I32 = jnp.int32

N_META = 16
N_HEADS = 16
HEAD_DIM = 64
N_GROUPS = 4
EXPERTS_PER_GROUP = 8
N_EXPERTS = N_GROUPS * EXPERTS_PER_GROUP
EPS = 1e-6
NEG = -1e30
LOG2E = 1.4426950408889634
BOUND_LANE = HEAD_DIM + 6
SAFE_BOUND_LOG2 = 50.0

LANES = 128
SEQ_ALIGN = 256
ROUTE_LANES = 4 + N_EXPERTS

TM_INPROJ = 1536
TN_INPROJ = 2048
TR_PREP = 768
CUMSUM_ROWS = 128
TQ_ATTN = 768
TK_ATTN = 256
HEADS_PER_STEP = 4
TM_MERGE = 768
TP_POS = 1536
BM_EXPERT = 512
TC_COMBINE = 1408

SC_CORES = 2
SC_SUBCORES = 16
SC_WORKERS = SC_CORES * SC_SUBCORES
SC_CHUNK = 176

VMEM_LIMIT = 56 * 1024 * 1024


def _cparams(sem, vmem=VMEM_LIMIT):
    return pltpu.CompilerParams(dimension_semantics=sem, vmem_limit_bytes=vmem)


def _pack_bf16_halves(x):
    bits = lax.bitcast_convert_type(x.astype(F32), I32)
    half = x.shape[1] // 2
    return lax.shift_right_logical(bits[:, :half], 16) | (bits[:, half:] & jnp.int32(-65536))


def _unpack_bf16_halves(w):
    return (lax.bitcast_convert_type(w << 16, F32),
            lax.bitcast_convert_type(w & jnp.int32(-65536), F32))


def _inproj_kernel(x_ref, g_ref, w_ref, wf_ref, o_ref, f_ref, hn_ref):
    @pl.when(pl.program_id(1) == 0)
    def _():
        x = x_ref[...]
        r = lax.rsqrt(jnp.mean(x * x, axis=-1, keepdims=True) + EPS)
        hn = (x * r * g_ref[...]).astype(BF16)
        hn_ref[...] = hn
        f_ref[...] = jnp.dot(hn, wf_ref[...], preferred_element_type=F32)

    o_ref[...] = jnp.dot(hn_ref[...], w_ref[...], preferred_element_type=F32).astype(BF16)


def _inproj(h, g, w_all, w_f):
    n, d = h.shape
    cols = w_all.shape[1]
    tm, tn = TM_INPROJ, TN_INPROJ
    return pl.pallas_call(
        _inproj_kernel,
        grid=(n // tm, cols // tn),
        in_specs=[
            pl.BlockSpec((tm, d), lambda i, j: (i, 0)),
            pl.BlockSpec((1, d), lambda i, j: (0, 0)),
            pl.BlockSpec((d, tn), lambda i, j: (0, j)),
            pl.BlockSpec((d, LANES), lambda i, j: (0, 0)),
        ],
        out_specs=[
            pl.BlockSpec((tm, tn), lambda i, j: (i, j)),
            pl.BlockSpec((tm, LANES), lambda i, j: (i, 0)),
        ],
        out_shape=[
            jax.ShapeDtypeStruct((n, cols), BF16),
            jax.ShapeDtypeStruct((n, LANES), F32),
        ],
        scratch_shapes=[pltpu.VMEM((tm, d), BF16)],
        compiler_params=_cparams(("parallel", "arbitrary")),
        name="inproj",
    )(h, g, w_all, w_f)


def _split3(c):
    hi = c.astype(BF16).astype(F32)
    r1 = c - hi
    mid = r1.astype(BF16).astype(F32)
    lo = (r1 - mid).astype(BF16).astype(F32)
    return hi, mid, lo


def _prep_kernel(q_ref, k_ref, v_ref, f_ref, bf_ref, gq_ref, gk_ref, hsel_ref, hexp_ref,
                 selq_ref, selk_ref,
                 qa_ref, ka_ref, va_ref, carry_ref):
    tr = f_ref.shape[0]

    @pl.when(pl.program_id(1) == 0)
    def _():
        carry_ref[...] = jnp.zeros_like(carry_ref)

    f = f_ref[...] + bf_ref[...]
    lf = jnp.minimum(f, 0.0) - jnp.log1p(jnp.exp(-jnp.abs(f)))
    sub = min(tr, CUMSUM_ROWS)
    row = lax.broadcasted_iota(I32, (sub, sub), 0)
    col = lax.broadcasted_iota(I32, (sub, sub), 1)
    tril = (col <= row).astype(F32)
    carry = carry_ref[...]
    parts = []
    for s0 in range(0, tr, sub):
        part = jnp.dot(tril, lf[s0:s0 + sub], precision=lax.Precision.HIGHEST,
                       preferred_element_type=F32) + carry
        carry = part[sub - 1:sub, :]
        parts.append(part)
    c = jnp.concatenate(parts, axis=0)
    carry_ref[...] = carry
    c_hi, c_mid, c_lo = _split3(c * LOG2E)

    lane = lax.broadcasted_iota(I32, (tr, LANES), 1)
    low = lane < HEAD_DIM
    ones_v = (lane == HEAD_DIM).astype(F32)

    heads = lane < N_HEADS
    c3 = (jnp.where(heads, c_hi, 0.0)
          + pltpu.roll(jnp.where(heads, c_mid, 0.0), N_HEADS, 1)
          + pltpu.roll(jnp.where(heads, c_lo, 0.0), 2 * N_HEADS, 1)
          + (lane == LANES - 1).astype(F32)).astype(BF16)

    def inv_rms(x):
        ss = jnp.dot((x * x).astype(BF16), hsel_ref[...], preferred_element_type=F32)
        r = lax.rsqrt(ss * (1.0 / HEAD_DIM) + EPS)
        r_hi = r.astype(BF16)
        r_lo = (r - r_hi.astype(F32)).astype(BF16)
        return jnp.dot(jnp.concatenate([r_hi, r_lo], axis=1), hexp_ref[...],
                       preferred_element_type=F32)

    rq = inv_rms(q_ref[...].astype(F32))
    rk = inv_rms(k_ref[...].astype(F32))

    for p in range(N_HEADS // 2):
        sl = slice(p * LANES, (p + 1) * LANES)
        e, o = 2 * p, 2 * p + 1
        qn = q_ref[:, sl].astype(F32) * rq[:, sl] * gq_ref[...]
        kn = k_ref[:, sl].astype(F32) * rk[:, sl] * gk_ref[...]
        vv = v_ref[:, sl].astype(F32)
        pc = slice(2 * p * LANES, (2 * p + 2) * LANES)
        ext_q = jnp.dot(c3, selq_ref[:, pc], preferred_element_type=F32)
        ext_k = jnp.dot(c3, selk_ref[:, pc], preferred_element_type=F32)
        for h in (e, o):
            if h == e:
                qh, kh, vh = qn, kn, vv
            else:
                qh = pltpu.roll(qn, HEAD_DIM, 1)
                kh = pltpu.roll(kn, HEAD_DIM, 1)
                vh = pltpu.roll(vv, HEAD_DIM, 1)
            hs = slice((h - e) * LANES, (h - e + 1) * LANES)
            qa_ref[0, h] = jnp.where(low, qh, ext_q[:, hs]).astype(BF16)
            ka_ref[0, h] = jnp.where(low, kh, ext_k[:, hs]).astype(BF16)
            va_ref[0, h] = jnp.where(low, vh, ones_v).astype(BF16)


def _prep_constants():
    feat = jnp.arange(N_HEADS * HEAD_DIM)
    hsel = (feat[:, None] // HEAD_DIM == jnp.arange(LANES)[None, :]).astype(BF16)
    hexp = jnp.concatenate([hsel.T, hsel.T], axis=0)
    rows = jnp.arange(LANES)
    part, src = rows // N_HEADS, rows % N_HEADS
    cols = jnp.arange(N_HEADS * LANES)
    head, off = cols // LANES, cols % LANES - HEAD_DIM
    is_head = (src[:, None] == head[None, :]) & (part[:, None] < 3)
    const_row = (rows == LANES - 1)[:, None]
    sel_q = (is_head & (off[None, :] == part[:, None])).astype(F32) \
        + (const_row & (off[None, :] >= 3) & (off[None, :] < 6)).astype(F32)
    sel_k = -(is_head & (off[None, :] == part[:, None] + 3)).astype(F32) \
        + (const_row & (((off[None, :] >= 0) & (off[None, :] < 3))
                        | (off[None, :] == BOUND_LANE - HEAD_DIM))).astype(F32)
    return hsel, hexp, sel_q.astype(BF16), sel_k.astype(BF16)


def _prep(proj, f, bf, gq2, gk2, batch, lp):
    tr = TR_PREP
    nr = lp // tr
    hsel, hexp, sel_q, sel_k = _prep_constants()
    aug = jax.ShapeDtypeStruct((batch, N_HEADS, lp, LANES), BF16)
    aug_spec = pl.BlockSpec((1, N_HEADS, tr, LANES), lambda b, r: (b, 0, r, 0))
    vec = pl.BlockSpec((1, LANES), lambda b, r: (0, 0))
    full = lambda a: pl.BlockSpec(a.shape, lambda b, r: (0, 0))
    return pl.pallas_call(
        _prep_kernel,
        grid=(batch, nr),
        in_specs=[
            pl.BlockSpec((tr, 1024), lambda b, r: (b * nr + r, 3)),
            pl.BlockSpec((tr, 1024), lambda b, r: (b * nr + r, 4)),
            pl.BlockSpec((tr, 1024), lambda b, r: (b * nr + r, 5)),
            pl.BlockSpec((tr, LANES), lambda b, r: (b * nr + r, 0)),
            vec, vec, vec, full(hsel), full(hexp), full(sel_q), full(sel_k),
        ],
        out_specs=[aug_spec, aug_spec, aug_spec],
        out_shape=[aug, aug, aug],
        scratch_shapes=[pltpu.VMEM((1, LANES), F32)],
        compiler_params=_cparams(("parallel", "arbitrary")),
        name="prep",
    )(proj, proj, proj, f, bf, gq2, gk2, hsel, hexp, sel_q, sel_k)


def _attn_kernel(qa_ref, ka_ref, va_ref, o_ref, acc_ref, kmax_ref, s_ref, *, tq, tk):
    i = pl.program_id(2)
    nh = qa_ref.shape[1]
    lp = ka_ref.shape[2]
    assert tq % tk == 0 and nh % 2 == 0
    n_diag = tq // tk
    n_full = i * n_diag
    lane = lax.broadcasted_iota(I32, (tq, LANES), 1)
    low = lane < HEAD_DIM
    nt = (((1,), (1,)), ((), ()))

    @pl.when(i == 0)
    def _():
        feat = (lax.broadcasted_iota(I32, (LANES, LANES), 0) < HEAD_DIM).astype(BF16)

        def chunk(r, best):
            out = []
            for hh in range(nh):
                kk = ka_ref[0, hh, pl.ds(pl.multiple_of(r * tk, tk), tk), :]
                n2 = jnp.dot(kk * kk, feat, preferred_element_type=F32)
                out.append(jnp.maximum(best[hh], jnp.max(n2.reshape(tk // 8, 8, LANES), axis=0)))
            return tuple(out)

        best = lax.fori_loop(0, lp // tk, chunk, (jnp.zeros((8, LANES), F32),) * nh)
        for hh in range(nh):
            kmax_ref[hh] = jnp.broadcast_to(1.01 * jnp.max(best[hh], axis=0, keepdims=True),
                                            (8, LANES))

    def kv(hh, j):
        start = pl.multiple_of(j * tk, tk)
        return (ka_ref[0, hh, pl.ds(start, tk), :], va_ref[0, hh, pl.ds(start, tk), :])

    qb = []
    worst = jnp.zeros((1, 1), F32)
    feat_q = (lax.broadcasted_iota(I32, (LANES, LANES), 0) < HEAD_DIM).astype(BF16)
    for hh in range(nh):
        q = qa_ref[0, hh]
        qn2 = 1.01 * jnp.dot(q * q, feat_q, preferred_element_type=F32)
        bound = jnp.sqrt(qn2 * kmax_ref[hh][0:1, :]) + 0.25
        qb.append(jnp.where(lane == BOUND_LANE, -bound, q.astype(F32)).astype(BF16))
        worst = jnp.maximum(worst, jnp.max(bound, axis=0, keepdims=True)[:, 0:1])
    safe = worst[0, 0] <= SAFE_BOUND_LOG2

    @pl.when(safe)
    def _():
        acc_ref[...] = jnp.zeros_like(acc_ref)

        skew = (lax.broadcasted_iota(I32, (tq, tk), 1) - lax.broadcasted_iota(I32, (tq, tk), 0))
        last = lp // tk - 1

        def logits(j, buf, r0=0):
            jj = jnp.minimum(j, last)
            for hh in range(nh):
                k, _ = kv(hh, jj)
                s = lax.dot_general(qb[hh][r0:], k, nt, preferred_element_type=F32)
                s_ref[buf, hh, r0:, :] = jnp.where(skew[r0:] <= i * tq - j * tk, s, NEG)

        def consume(j, buf, r0=0):
            jj = jnp.minimum(j, last)
            for hh in range(nh):
                _, v = kv(hh, jj)
                p = jnp.exp2(s_ref[buf, hh, r0:, :]).astype(BF16)
                acc_ref[hh, r0:, :] += jnp.dot(p, v, preferred_element_type=F32)

        def step(j, cur, nxt):
            logits(j + 1, nxt)
            consume(j, cur)

        def quad(r, carry):
            for u in range(4):
                step(4 * r + u, u % 2, 1 - u % 2)
            return carry

        n_u = n_full + 1
        logits(0, 0)
        quads = (n_u - 1) // 4
        lax.fori_loop(0, quads, quad, 0)
        j4 = 4 * quads
        pairs = (n_u - j4) // 2

        def pair(r, carry):
            for u in range(2):
                step(j4 + 2 * r + u, u, 1 - u)
            return carry

        lax.fori_loop(0, pairs, pair, 0)
        tail = j4 + 2 * pairs

        def finish(first_d):
            pending = lambda: consume(tail, 0)
            buf = 1
            for dd in range(first_d, n_diag):
                logits(n_full + dd, buf, dd * tk)
                pending()
                pending = lambda dd=dd, buf=buf: consume(n_full + dd, buf, dd * tk)
                buf = 1 - buf
            pending()

        @pl.when(tail == n_full)
        def _():
            finish(1)

        @pl.when(tail != n_full)
        def _():
            finish(2)

    @pl.when(jnp.logical_not(safe))
    def _():
        for hh in range(nh):
            q = qa_ref[0, hh]

            def tile(j, carry, masked, hh=hh, q=q):
                m, acc = carry
                k, v = kv(hh, j)
                s = lax.dot_general(q, k, nt, preferred_element_type=F32)
                if masked:
                    qpos = i * tq + lax.broadcasted_iota(I32, (tq, tk), 0)
                    kpos = j * tk + lax.broadcasted_iota(I32, (tq, tk), 1)
                    s = jnp.where(kpos <= qpos, s, NEG)
                m_new = jnp.maximum(m, jnp.max(s, axis=1, keepdims=True))
                acc = jnp.exp2(m - m_new) * acc + jnp.dot(
                    jnp.exp2(s - m_new).astype(BF16), v, preferred_element_type=F32)
                return m_new, acc

            carry = (jnp.full((tq, 1), NEG, F32), jnp.zeros((tq, LANES), F32))
            carry = lax.fori_loop(0, n_full, functools.partial(tile, masked=False), carry)
            for d in range(n_diag):
                carry = tile(n_full + d, carry, True)
            acc_ref[hh] = carry[1]

    for pr in range(nh // 2):
        a0 = acc_ref[2 * pr]
        a1 = acc_ref[2 * pr + 1]
        o0 = a0 / a0[:, HEAD_DIM:HEAD_DIM + 1]
        o1 = a1 / a1[:, HEAD_DIM:HEAD_DIM + 1]
        o_ref[:, pr * LANES:(pr + 1) * LANES] = jnp.where(
            low, o0, pltpu.roll(o1, HEAD_DIM, 1)).astype(BF16)


def _attention(qa, ka, va):
    batch, _, lp, _ = qa.shape
    tq, tk, nh = TQ_ATTN, TK_ATTN, HEADS_PER_STEP
    nq = lp // tq
    return pl.pallas_call(
        functools.partial(_attn_kernel, tq=tq, tk=tk),
        grid=(batch, N_HEADS // nh, nq),
        in_specs=[
            pl.BlockSpec((1, nh, tq, LANES), lambda b, h, i: (b, h, i, 0)),
            pl.BlockSpec((1, nh, lp, LANES), lambda b, h, i: (b, h, 0, 0)),
            pl.BlockSpec((1, nh, lp, LANES), lambda b, h, i: (b, h, 0, 0)),
        ],
        out_specs=pl.BlockSpec((tq, nh * HEAD_DIM), lambda b, h, i: (b * nq + i, h)),
        out_shape=jax.ShapeDtypeStruct((batch * lp, N_HEADS * HEAD_DIM), BF16),
        scratch_shapes=[pltpu.VMEM((nh, tq, LANES), F32), pltpu.VMEM((nh, 8, LANES), F32),
                        pltpu.VMEM((2, nh, tq, tk), F32)],
        compiler_params=_cparams(("parallel", "parallel", "arbitrary")),
        name="attn",
    )(qa, ka, va)


def _merge_kernel(b_ref, c_ref, u_ref, ch_ref, uh_ref, ga_ref, gb_ref, at_ref, h_ref,
                  cw_ref, wc_ref, wa_ref, wm_ref, g2_ref, wr2_ref, br_ref,
                  ho_ref, xp_ref, rt_ref, *, tiles_per_seq):
    tm, d = h_ref.shape
    i = pl.program_id(0)

    z = c_ref[...].astype(F32) * u_ref[...].astype(F32)
    hv = (i % tiles_per_seq != 0).astype(F32)
    zh = ch_ref[...].astype(F32) * uh_ref[...].astype(F32) * hv
    hr = zh.shape[0]
    zm1 = zh[hr - 1:hr, :]
    zm2 = zh[hr - 2:hr - 1, :]
    rowid = lax.broadcasted_iota(I32, (tm, d), 0)
    zp1 = jnp.where(rowid == 0, zm1, pltpu.roll(z, 1, 0))
    zp2 = jnp.where(rowid == 0, zm2, jnp.where(rowid == 1, zm1, pltpu.roll(z, 2, 0)))
    cw = cw_ref[...]
    yc = b_ref[...].astype(F32) * (zp2 * cw[0:1, :] + zp1 * cw[1:2, :] + z * cw[2:3, :])

    y_a = jnp.dot(yc.astype(BF16), wc_ref[...], preferred_element_type=F32)
    y_b = jnp.dot(at_ref[...], wa_ref[...], preferred_element_type=F32)
    mix = (jax.nn.sigmoid(ga_ref[...].astype(F32)) * y_a
           + jax.nn.sigmoid(gb_ref[...].astype(F32)) * y_b)
    hnew = h_ref[...] + jnp.dot(mix.astype(BF16), wm_ref[...], preferred_element_type=F32)
    ho_ref[...] = hnew

    r = lax.rsqrt(jnp.mean(hnew * hnew, axis=-1, keepdims=True) + EPS)
    xn = hnew * r * g2_ref[...]
    x_hi = xn.astype(BF16)
    xp_ref[...] = _pack_bf16_halves(x_hi)

    x_lo = (xn - x_hi.astype(F32)).astype(BF16)
    both =jnp.dot(x_hi, wr2_ref[...], preferred_element_type=F32)
    logits = (both[:, :LANES] + both[:, LANES:]
              + jnp.dot(x_lo, wr2_ref[:, :LANES], preferred_element_type=F32)) + br_ref[...]

    lanei = lax.broadcasted_iota(I32, (tm, LANES), 1)
    lane = lanei.astype(F32)
    big = jnp.float32(1e9)
    is_g = lanei < N_GROUPS
    is_e = (lanei >= N_GROUPS) & (lanei < ROUTE_LANES)
    gl = jnp.where(is_g, logits, NEG)
    gmax = jnp.max(gl, axis=1, keepdims=True)
    gidx = jnp.min(jnp.where(gl == gmax, lane, big), axis=1, keepdims=True)
    gsum = jnp.sum(jnp.where(is_g, jnp.exp(gl - gmax), 0.0), axis=1, keepdims=True)
    grp = ((lanei - N_GROUPS) >> 3).astype(F32)
    el = jnp.where(is_e & (grp == gidx), logits, NEG)
    emax = jnp.max(el, axis=1, keepdims=True)
    i1 = jnp.min(jnp.where(el == emax, lane, big), axis=1, keepdims=True)
    el2 = jnp.where(lane == i1, NEG, el)
    e2max = jnp.max(el2, axis=1, keepdims=True)
    i2 = jnp.min(jnp.where(el2 == e2max, lane, big), axis=1, keepdims=True)
    t = jnp.exp(e2max - emax)
    w1 = 1.0 / (gsum * (1.0 + t))
    w2 = w1 * t
    rt_ref[...] = jnp.where(lanei == 0, i1 - N_GROUPS,
                            jnp.where(lanei == 1, i2 - N_GROUPS,
                                      jnp.where(lanei == 2, w1,
                                                jnp.where(lanei == 3, w2, 0.0))))


def _merge(proj, attn, h, cw, wc, wa, wm, g2, wr2, br, lp):
    n, d = h.shape
    tm = TM_MERGE
    halo = 16
    hb = tm // halo
    row = lambda c: pl.BlockSpec((tm, d), lambda i: (i, c))
    halo_spec = lambda c: pl.BlockSpec((halo, d), lambda i: (jnp.maximum(i * hb - 1, 0), c))
    full = lambda r, c: pl.BlockSpec((r, c), lambda i: (0, 0))
    return pl.pallas_call(
        functools.partial(_merge_kernel, tiles_per_seq=lp // tm),
        grid=(n // tm,),
        in_specs=[
            row(0), row(1), row(2), halo_spec(1), halo_spec(2), row(6), row(7),
            pl.BlockSpec((tm, d), lambda i: (i, 0)),
            pl.BlockSpec((tm, d), lambda i: (i, 0)),
            full(8, d), full(d, d), full(d, d), full(d, d), full(1, d),
            full(d, 2 * LANES), full(1, LANES),
        ],
        out_specs=[
            pl.BlockSpec((tm, d), lambda i: (i, 0)),
            pl.BlockSpec((tm, d // 2), lambda i: (i, 0)),
            pl.BlockSpec((tm, LANES), lambda i: (i, 0)),
        ],
        out_shape=[
            jax.ShapeDtypeStruct((n, d), F32),
            jax.ShapeDtypeStruct((n, d // 2), I32),
            jax.ShapeDtypeStruct((n, LANES), F32),
        ],
        compiler_params=_cparams(("parallel",)),
        name="merge",
    )(proj, proj, proj, proj, proj, proj, proj, attn, h,
      cw, wc, wa, wm, g2, wr2, br)


def _pos_kernel(rt_ref, pos_ref, blk_ref, cnt_ref, run_ref, pst_ref, *, bm):
    ph = pl.program_id(0)
    c = pl.program_id(1)
    tp = rt_ref.shape[0]
    nbp = blk_ref.shape[0]
    lane = lax.broadcasted_iota(I32, (tp, LANES), 1).astype(F32)
    rt = rt_ref[...]
    sel0 = lane == rt[:, 0:1]
    sel1 = lane == rt[:, 1:2]
    both = sel0.astype(F32) + sel1.astype(F32)
    colsum = jnp.sum(both, axis=0, keepdims=True)

    @pl.when((ph == 0) & (c == 0))
    def _():
        cnt_ref[...] = jnp.zeros_like(cnt_ref)

    @pl.when(ph == 0)
    def _():
        cnt_ref[...] += colsum

    @pl.when((ph == 1) & (c == 0))
    def _():
        nblk = jnp.floor((cnt_ref[...] + (bm - 1)) * (1.0 / bm))
        r0 = lax.broadcasted_iota(I32, (LANES, LANES), 0)
        c0 = lax.broadcasted_iota(I32, (LANES, LANES), 1)
        upper = (r0 < c0).astype(BF16)
        nblk8 = jnp.broadcast_to(nblk, (8, LANES)).astype(BF16)
        pstb = jnp.dot(nblk8, upper, preferred_element_type=F32)[0:1, :]
        pst_ref[...] = pstb * bm
        run_ref[...] = jnp.zeros_like(run_ref)
        pend = pstb + nblk
        bidx = lax.broadcasted_iota(I32, (nbp, LANES), 0).astype(F32)
        l2 = lax.broadcasted_iota(I32, (nbp, LANES), 1)
        started = ((pend <= bidx) & (l2 < N_EXPERTS)).astype(F32)
        eid = jnp.minimum(jnp.sum(started, axis=1, keepdims=True), N_EXPERTS - 1.0)
        mine = l2.astype(F32) == eid
        cnt_e = jnp.sum(jnp.where(mine, cnt_ref[...], 0.0), axis=1, keepdims=True)
        first = jnp.sum(jnp.where(mine, pstb, 0.0), axis=1, keepdims=True)
        nvalid = jnp.clip(cnt_e - (bidx[:, 0:1] - first) * bm, 0.0, float(bm))
        blk_ref[...] = jnp.where(l2 == 0, eid, jnp.where(l2 == 1, nvalid, 0.0)).astype(I32)

    @pl.when(ph == 1)
    def _():
        r0 = lax.broadcasted_iota(I32, (tp, tp), 0)
        c0 = lax.broadcasted_iota(I32, (tp, tp), 1)
        lower = (c0 < r0).astype(BF16)
        before = jnp.dot(lower, both.astype(BF16), preferred_element_type=F32)
        base = before + run_ref[...] + pst_ref[...]
        d0 = jnp.sum(jnp.where(sel0, base, 0.0), axis=1, keepdims=True)
        d1 = jnp.sum(jnp.where(sel1, base, 0.0), axis=1, keepdims=True)
        pos_ref[...] = jnp.where(lane == 0.0, d0, jnp.where(lane == 1.0, d1, 0.0)).astype(I32)
        run_ref[...] += colsum


def _positions(route, nbp):
    n = route.shape[0]
    tp = TP_POS
    return pl.pallas_call(
        functools.partial(_pos_kernel, bm=BM_EXPERT),
        grid=(2, n // tp),
        in_specs=[pl.BlockSpec((tp, LANES), lambda ph, c: (c, 0))],
        out_specs=[
            pl.BlockSpec((tp, LANES), lambda ph, c: (c * ph, 0)),
            pl.BlockSpec((nbp, LANES), lambda ph, c: (0, 0)),
        ],
        out_shape=[
            jax.ShapeDtypeStruct((n, LANES), I32),
            jax.ShapeDtypeStruct((nbp, LANES), I32),
        ],
        scratch_shapes=[pltpu.VMEM((1, LANES), F32)] * 3,
        compiler_params=_cparams(("arbitrary", "arbitrary")),
        name="positions",
    )(route)


def _sc_mesh():
    return plsc.VectorSubcoreMesh(core_axis_name="c", subcore_axis_name="s",
                                  num_cores=SC_CORES, num_subcores=SC_SUBCORES)


def _sc_worker_base(rows_per_worker):
    return (lax.axis_index("s") * SC_CORES + lax.axis_index("c")) * rows_per_worker


def _dispatch(dest, xp, p_rows):
    n, half = xp.shape
    per_worker = n // SC_WORKERS
    chunk = SC_CHUNK
    assert n % (8 * SC_WORKERS) == 0 and per_worker % chunk == 0

    def body(xp_hbm, dest_hbm, xs_hbm, i0_v, i1_v, rows_v):
        base = _sc_worker_base(per_worker)

        @pl.loop(0, per_worker // chunk)
        def _(c):
            off = pl.multiple_of(base + c * chunk, 8)
            pltpu.sync_copy(dest_hbm.at[pl.ds(off, chunk)], i0_v)
            pltpu.sync_copy(dest_hbm.at[pl.ds(n + off, chunk)], i1_v)
            pltpu.sync_copy(xp_hbm.at[pl.ds(off, chunk)], rows_v)
            pltpu.sync_copy(rows_v, xs_hbm.at[i0_v])
            pltpu.sync_copy(rows_v, xs_hbm.at[i1_v])

    return pl.kernel(
        body, mesh=_sc_mesh(),
        out_type=jax.ShapeDtypeStruct((p_rows, half), xp.dtype),
        scratch_types=[pltpu.VMEM((chunk,), I32), pltpu.VMEM((chunk,), I32),
                       pltpu.VMEM((chunk, half), xp.dtype)],
        name="dispatch",
    )(xp, dest)


def _gather_rows(ys, dest):
    a_rows = dest.shape[0]
    d = ys.shape[1]
    per_worker = a_rows // SC_WORKERS
    chunk = SC_CHUNK
    assert a_rows % (8 * SC_WORKERS) == 0 and per_worker % chunk == 0

    def body(ys_hbm, dest_hbm, out_hbm, idx_v, rows_v):
        base = _sc_worker_base(per_worker)

        @pl.loop(0, per_worker // chunk)
        def _(c):
            off = pl.multiple_of(base + c * chunk, 8)
            pltpu.sync_copy(dest_hbm.at[pl.ds(off, chunk)], idx_v)
            pltpu.sync_copy(ys_hbm.at[idx_v], rows_v)
            pltpu.sync_copy(rows_v, out_hbm.at[pl.ds(off, chunk)])

    return pl.kernel(
        body, mesh=_sc_mesh(),
        out_type=jax.ShapeDtypeStruct((a_rows, d), ys.dtype),
        scratch_types=[pltpu.VMEM((chunk,), I32), pltpu.VMEM((chunk, d), ys.dtype)],
        name="gather_rows",
    )(ys, dest)


def _expert_kernel(eid_ref, nvalid_ref, xs_ref, wg_ref, wu_ref, wd_ref, ys_ref,
                   wgb_ref, wub_ref, wdb_ref):
    b = pl.program_id(0)
    nvalid = nvalid_ref[b]

    @pl.when((b == 0) | (eid_ref[b] != eid_ref[jnp.maximum(b - 1, 0)]))
    def _():
        wgb_ref[...] = wg_ref[...].astype(BF16)
        wub_ref[...] = wu_ref[...].astype(BF16)
        wdb_ref[...] = wd_ref[...].astype(BF16)

    @pl.when(nvalid != 0)
    def _():
        rows = lax.broadcasted_iota(I32, xs_ref.shape, 0)
        w = jnp.where(rows < nvalid, xs_ref[...], 0)
        half = w.shape[1]
        lo, hi = (v.astype(BF16) for v in _unpack_bf16_halves(w))

        def proj(w_ref):
            return (jnp.dot(lo, w_ref[:half, :], preferred_element_type=F32)
                    + jnp.dot(hi, w_ref[half:, :], preferred_element_type=F32))

        hg = proj(wgb_ref)
        hu = proj(wub_ref)
        act = (hg * jax.nn.sigmoid(hg) * hu).astype(BF16)
        y = jnp.dot(act, wdb_ref[...], preferred_element_type=F32)
        ys_ref[...] = _pack_bf16_halves(y.astype(BF16))

    @pl.when(nvalid == 0)
    def _():
        ys_ref[...] = jnp.zeros_like(ys_ref)


def _experts(eid, nvalid, xs, wg, wu, wd, layer):
    p, half = xs.shape
    d = 2 * half
    de = wg.shape[3]
    bm = BM_EXPERT
    return pl.pallas_call(
        _expert_kernel,
        grid_spec=pltpu.PrefetchScalarGridSpec(
            num_scalar_prefetch=2,
            grid=(p // bm,),
            in_specs=[
                pl.BlockSpec((bm, half), lambda b, e, v: (b, 0)),
                pl.BlockSpec((None, None, d, de), lambda b, e, v: (layer, e[b], 0, 0)),
                pl.BlockSpec((None, None, d, de), lambda b, e, v: (layer, e[b], 0, 0)),
                pl.BlockSpec((None, None, de, d), lambda b, e, v: (layer, e[b], 0, 0)),
            ],
            out_specs=pl.BlockSpec((bm, half), lambda b, e, v: (b, 0)),
            scratch_shapes=[pltpu.VMEM((d, de), BF16), pltpu.VMEM((d, de), BF16),
                            pltpu.VMEM((de, d), BF16)],
        ),
        out_shape=jax.ShapeDtypeStruct((p, half), I32),
        compiler_params=_cparams(("arbitrary",)),
        name="experts",
    )(eid, nvalid, xs, wg, wu, wd)


def _combine_kernel(h_ref, rt_ref, g0_ref, g1_ref, o_ref):
    rt = rt_ref[...]
    half = g0_ref.shape[1]
    lo0, hi0 = _unpack_bf16_halves(g0_ref[...])
    lo1, hi1 = _unpack_bf16_halves(g1_ref[...])
    w0, w1 = rt[:, 2:3], rt[:, 3:4]
    o_ref[:, :half] = h_ref[:, :half] + w0 * lo0 + w1 * lo1
    o_ref[:, half:] = h_ref[:, half:] + w0 * hi0 + w1 * hi1


def _combine(h, route, g):
    n, d = h.shape
    half = g.shape[1]
    tc = TC_COMBINE
    nt = n // tc
    return pl.pallas_call(
        _combine_kernel,
        grid=(nt,),
        in_specs=[pl.BlockSpec((tc, d), lambda i: (i, 0)),
                  pl.BlockSpec((tc, LANES), lambda i: (i, 0)),
                  pl.BlockSpec((tc, half), lambda i: (i, 0)),
                  pl.BlockSpec((tc, half), lambda i: (nt + i, 0))],
        out_specs=pl.BlockSpec((tc, d), lambda i: (i, 0)),
        out_shape=jax.ShapeDtypeStruct((n, d), F32),
        compiler_params=_cparams(("parallel",)),
        name="combine",
    )(h, route, g, g)


def _pad_lanes(v, width=LANES):
    return jnp.pad(v, [(0, 0)] * (v.ndim - 1) + [(0, width - v.shape[-1])])


def kernel(x, meta_tokens, norm1_g, w_in, b_forget, conv_w, w_conv_out, q_norm_g, k_norm_g,
           w_att_out, w_merge_out, norm2_g, w_router_group, b_router_group, w_router_expert,
           b_router_expert, w_exp_gate, w_exp_up, w_exp_down):
    batch, seq, d = x.shape
    depth = w_in.shape[0]
    length = seq + N_META
    lp = -(-length // SEQ_ALIGN) * SEQ_ALIGN
    n = batch * lp
    assert d == N_HEADS * HEAD_DIM == 1024
    assert lp % TQ_ATTN == 0 and lp % TM_MERGE == 0 and lp % TR_PREP == 0
    assert n % TM_INPROJ == 0 and n % TC_COMBINE == 0 and n % TP_POS == 0

    n_blocks = -(-(2 * n + N_EXPERTS * (BM_EXPERT - 1)) // BM_EXPERT)
    nbp = -(-n_blocks // 8) * 8
    p_rows = n_blocks * BM_EXPERT

    meta = jnp.broadcast_to(meta_tokens[None].astype(x.dtype), (batch, N_META, d))
    h = jnp.concatenate([meta, x, jnp.zeros((batch, lp - length, d), x.dtype)], axis=1)
    h = h.reshape(n, d)

    scale = HEAD_DIM ** -0.5
    for l in range(depth):
        w = w_in[l]
        w_all = jnp.concatenate([w[:, :6 * d], w[:, 6 * d + N_HEADS:]], axis=1).astype(BF16)
        w_f = _pad_lanes(w[:, 6 * d:6 * d + N_HEADS]).astype(BF16)
        proj, f = _inproj(h, norm1_g[l][None], w_all, w_f)

        bf = _pad_lanes(b_forget[l][None])
        gq2 = jnp.tile(q_norm_g[l], 2)[None] * (scale * LOG2E)
        gk2 = jnp.tile(k_norm_g[l], 2)[None]
        qa, ka, va = _prep(proj, f, bf, gq2, gk2, batch, lp)
        attn = _attention(qa, ka, va)

        w_r = _pad_lanes(jnp.concatenate([w_router_group[l], w_router_expert[l]], axis=1))
        wr_hi = w_r.astype(BF16)
        wr2 = jnp.concatenate([wr_hi, (w_r - wr_hi.astype(F32)).astype(BF16)], axis=1)
        b_r = _pad_lanes(jnp.concatenate([b_router_group[l], b_router_expert[l]])[None])
        cw = jnp.pad(conv_w[l], ((0, 8 - conv_w.shape[1]), (0, 0)))
        h_mid, xp, route = _merge(
            proj, attn, h, cw, w_conv_out[l].astype(BF16), w_att_out[l].astype(BF16),
            w_merge_out[l].astype(BF16), norm2_g[l][None], wr2, b_r, lp)

        pos, blk = _positions(route, nbp)
        dest = jnp.concatenate([pos[:, 0], pos[:, 1]])
        xs = _dispatch(dest, xp, p_rows)
        ys = _experts(blk[:n_blocks, 0], blk[:n_blocks, 1], xs,
                      w_exp_gate, w_exp_up, w_exp_down, l)
        h = _combine(h_mid, route, _gather_rows(ys, dest))

    return h.reshape(batch, lp, d)[:, N_META:length]
```

```python
import functools

import jax
import jax.numpy as jnp
from jax import lax
from jax.experimental import pallas as pl
from jax.experimental.pallas import tpu as pltpu
from jax.experimental.pallas import tpu_sc as plsc

F32 = jnp.float32
BF16 = jnp.bfloat16
I32 = jnp.int32

N_META = 16
N_HEADS = 16
HEAD_DIM = 64
N_GROUPS = 4
EXPERTS_PER_GROUP = 8
N_EXPERTS = N_GROUPS * EXPERTS_PER_GROUP
EPS = 1e-6
NEG = -1e30
LOG2E = 1.4426950408889634
BOUND_LANE = HEAD_DIM + 6
SAFE_BOUND_LOG2 = 50.0

LANES = 128
SEQ_ALIGN = 256
ROUTE_LANES = 4 + N_EXPERTS

TM_INPROJ = 1536
TN_INPROJ = 2048
TR_PREP = 768
CUMSUM_ROWS = 128
TQ_ATTN = 768
TK_ATTN = 256
HEADS_PER_STEP = 4
TM_MERGE = 768
TP_POS = 1536
BM_EXPERT = 512
TC_COMBINE = 1408

SC_CORES = 2
SC_SUBCORES = 16
SC_WORKERS = SC_CORES * SC_SUBCORES
SC_CHUNK = 176

VMEM_LIMIT = 56 * 1024 * 1024


def _cparams(sem, vmem=VMEM_LIMIT):
    return pltpu.CompilerParams(dimension_semantics=sem, vmem_limit_bytes=vmem)


def _pack_bf16_halves(x):
    bits = lax.bitcast_convert_type(x.astype(F32), I32)
    half = x.shape[1] // 2
    return lax.shift_right_logical(bits[:, :half], 16) | (bits[:, half:] & jnp.int32(-65536))


def _unpack_bf16_halves(w):
    return (lax.bitcast_convert_type(w << 16, F32),
            lax.bitcast_convert_type(w & jnp.int32(-65536), F32))


def _inproj_kernel(x_ref, g_ref, w_ref, wf_ref, o_ref, f_ref, hn_ref):
    @pl.when(pl.program_id(1) == 0)
    def _():
        x = x_ref[...]
        r = lax.rsqrt(jnp.mean(x * x, axis=-1, keepdims=True) + EPS)
        hn = (x * r * g_ref[...]).astype(BF16)
        hn_ref[...] = hn
        f_ref[...] = jnp.dot(hn, wf_ref[...], preferred_element_type=F32)

    o_ref[...] = jnp.dot(hn_ref[...], w_ref[...], preferred_element_type=F32).astype(BF16)


def _inproj(h, g, w_all, w_f):
    n, d = h.shape
    cols = w_all.shape[1]
    tm, tn = TM_INPROJ, TN_INPROJ
    return pl.pallas_call(
        _inproj_kernel,
        grid=(n // tm, cols // tn),
        in_specs=[
            pl.BlockSpec((tm, d), lambda i, j: (i, 0)),
            pl.BlockSpec((1, d), lambda i, j: (0, 0)),
            pl.BlockSpec((d, tn), lambda i, j: (0, j)),
            pl.BlockSpec((d, LANES), lambda i, j: (0, 0)),
        ],
        out_specs=[
            pl.BlockSpec((tm, tn), lambda i, j: (i, j)),
            pl.BlockSpec((tm, LANES), lambda i, j: (i, 0)),
        ],
        out_shape=[
            jax.ShapeDtypeStruct((n, cols), BF16),
            jax.ShapeDtypeStruct((n, LANES), F32),
        ],
        scratch_shapes=[pltpu.VMEM((tm, d), BF16)],
        compiler_params=_cparams(("parallel", "arbitrary")),
        name="inproj",
    )(h, g, w_all, w_f)


def _split3(c):
    hi = c.astype(BF16).astype(F32)
    r1 = c - hi
    mid = r1.astype(BF16).astype(F32)
    lo = (r1 - mid).astype(BF16).astype(F32)
    return hi, mid, lo


def _prep_kernel(q_ref, k_ref, v_ref, f_ref, bf_ref, gq_ref, gk_ref, hsel_ref, hexp_ref,
                 selq_ref, selk_ref,
                 qa_ref, ka_ref, va_ref, carry_ref):
    tr = f_ref.shape[0]

    @pl.when(pl.program_id(1) == 0)
    def _():
        carry_ref[...] = jnp.zeros_like(carry_ref)

    f = f_ref[...] + bf_ref[...]
    lf = jnp.minimum(f, 0.0) - jnp.log1p(jnp.exp(-jnp.abs(f)))
    sub = min(tr, CUMSUM_ROWS)
    row = lax.broadcasted_iota(I32, (sub, sub), 0)
    col = lax.broadcasted_iota(I32, (sub, sub), 1)
    tril = (col <= row).astype(F32)
    carry = carry_ref[...]
    parts = []
    for s0 in range(0, tr, sub):
        part = jnp.dot(tril, lf[s0:s0 + sub], precision=lax.Precision.HIGHEST,
                       preferred_element_type=F32) + carry
        carry = part[sub - 1:sub, :]
        parts.append(part)
    c = jnp.concatenate(parts, axis=0)
    carry_ref[...] = carry
    c_hi, c_mid, c_lo = _split3(c * LOG2E)

    lane = lax.broadcasted_iota(I32, (tr, LANES), 1)
    low = lane < HEAD_DIM
    ones_v = (lane == HEAD_DIM).astype(F32)

    heads = lane < N_HEADS
    c3 = (jnp.where(heads, c_hi, 0.0)
          + pltpu.roll(jnp.where(heads, c_mid, 0.0), N_HEADS, 1)
          + pltpu.roll(jnp.where(heads, c_lo, 0.0), 2 * N_HEADS, 1)
          + (lane == LANES - 1).astype(F32)).astype(BF16)

    def inv_rms(x):
        ss = jnp.dot((x * x).astype(BF16), hsel_ref[...], preferred_element_type=F32)
        r = lax.rsqrt(ss * (1.0 / HEAD_DIM) + EPS)
        r_hi = r.astype(BF16)
        r_lo = (r - r_hi.astype(F32)).astype(BF16)
        return jnp.dot(jnp.concatenate([r_hi, r_lo], axis=1), hexp_ref[...],
                       preferred_element_type=F32)

    rq = inv_rms(q_ref[...].astype(F32))
    rk = inv_rms(k_ref[...].astype(F32))

    for p in range(N_HEADS // 2):
        sl = slice(p * LANES, (p + 1) * LANES)
        e, o = 2 * p, 2 * p + 1
        qn = q_ref[:, sl].astype(F32) * rq[:, sl] * gq_ref[...]
        kn = k_ref[:, sl].astype(F32) * rk[:, sl] * gk_ref[...]
        vv = v_ref[:, sl].astype(F32)
        pc = slice(2 * p * LANES, (2 * p + 2) * LANES)
        ext_q = jnp.dot(c3, selq_ref[:, pc], preferred_element_type=F32)
        ext_k = jnp.dot(c3, selk_ref[:, pc], preferred_element_type=F32)
        for h in (e, o):
            if h == e:
                qh, kh, vh = qn, kn, vv
            else:
                qh = pltpu.roll(qn, HEAD_DIM, 1)
                kh = pltpu.roll(kn, HEAD_DIM, 1)
                vh = pltpu.roll(vv, HEAD_DIM, 1)
            hs = slice((h - e) * LANES, (h - e + 1) * LANES)
            qa_ref[0, h] = jnp.where(low, qh, ext_q[:, hs]).astype(BF16)
            ka_ref[0, h] = jnp.where(low, kh, ext_k[:, hs]).astype(BF16)
            va_ref[0, h] = jnp.where(low, vh, ones_v).astype(BF16)


def _prep_constants():
    feat = jnp.arange(N_HEADS * HEAD_DIM)
    hsel = (feat[:, None] // HEAD_DIM == jnp.arange(LANES)[None, :]).astype(BF16)
    hexp = jnp.concatenate([hsel.T, hsel.T], axis=0)
    rows = jnp.arange(LANES)
    part, src = rows // N_HEADS, rows % N_HEADS
    cols = jnp.arange(N_HEADS * LANES)
    head, off = cols // LANES, cols % LANES - HEAD_DIM
    is_head = (src[:, None] == head[None, :]) & (part[:, None] < 3)
    const_row = (rows == LANES - 1)[:, None]
    sel_q = (is_head & (off[None, :] == part[:, None])).astype(F32) \
        + (const_row & (off[None, :] >= 3) & (off[None, :] < 6)).astype(F32)
    sel_k = -(is_head & (off[None, :] == part[:, None] + 3)).astype(F32) \
        + (const_row & (((off[None, :] >= 0) & (off[None, :] < 3))
                        | (off[None, :] == BOUND_LANE - HEAD_DIM))).astype(F32)
    return hsel, hexp, sel_q.astype(BF16), sel_k.astype(BF16)


def _prep(proj, f, bf, gq2, gk2, batch, lp):
    tr = TR_PREP
    nr = lp // tr
    hsel, hexp, sel_q, sel_k = _prep_constants()
    aug = jax.ShapeDtypeStruct((batch, N_HEADS, lp, LANES), BF16)
    aug_spec = pl.BlockSpec((1, N_HEADS, tr, LANES), lambda b, r: (b, 0, r, 0))
    vec = pl.BlockSpec((1, LANES), lambda b, r: (0, 0))
    full = lambda a: pl.BlockSpec(a.shape, lambda b, r: (0, 0))
    return pl.pallas_call(
        _prep_kernel,
        grid=(batch, nr),
        in_specs=[
            pl.BlockSpec((tr, 1024), lambda b, r: (b * nr + r, 3)),
            pl.BlockSpec((tr, 1024), lambda b, r: (b * nr + r, 4)),
            pl.BlockSpec((tr, 1024), lambda b, r: (b * nr + r, 5)),
            pl.BlockSpec((tr, LANES), lambda b, r: (b * nr + r, 0)),
            vec, vec, vec, full(hsel), full(hexp), full(sel_q), full(sel_k),
        ],
        out_specs=[aug_spec, aug_spec, aug_spec],
        out_shape=[aug, aug, aug],
        scratch_shapes=[pltpu.VMEM((1, LANES), F32)],
        compiler_params=_cparams(("parallel", "arbitrary")),
        name="prep",
    )(proj, proj, proj, f, bf, gq2, gk2, hsel, hexp, sel_q, sel_k)


def _attn_kernel(qa_ref, ka_ref, va_ref, o_ref, acc_ref, kmax_ref, s_ref, *, tq, tk):
    i = pl.program_id(2)
    nh = qa_ref.shape[1]
    lp = ka_ref.shape[2]
    assert tq % tk == 0 and nh % 2 == 0
    n_diag = tq // tk
    n_full = i * n_diag
    lane = lax.broadcasted_iota(I32, (tq, LANES), 1)
    low = lane < HEAD_DIM
    nt = (((1,), (1,)), ((), ()))

    @pl.when(i == 0)
    def _():
        feat = (lax.broadcasted_iota(I32, (LANES, LANES), 0) < HEAD_DIM).astype(BF16)

        def chunk(r, best):
            out = []
            for hh in range(nh):
                kk = ka_ref[0, hh, pl.ds(pl.multiple_of(r * tk, tk), tk), :]
                n2 = jnp.dot(kk * kk, feat, preferred_element_type=F32)
                out.append(jnp.maximum(best[hh], jnp.max(n2.reshape(tk // 8, 8, LANES), axis=0)))
            return tuple(out)

        best = lax.fori_loop(0, lp // tk, chunk, (jnp.zeros((8, LANES), F32),) * nh)
        for hh in range(nh):
            kmax_ref[hh] = jnp.broadcast_to(1.01 * jnp.max(best[hh], axis=0, keepdims=True),
                                            (8, LANES))

    def kv(hh, j):
        start = pl.multiple_of(j * tk, tk)
        return (ka_ref[0, hh, pl.ds(start, tk), :], va_ref[0, hh, pl.ds(start, tk), :])

    qb = []
    worst = jnp.zeros((1, 1), F32)
    feat_q = (lax.broadcasted_iota(I32, (LANES, LANES), 0) < HEAD_DIM).astype(BF16)
    for hh in range(nh):
        q = qa_ref[0, hh]
        qn2 = 1.01 * jnp.dot(q * q, feat_q, preferred_element_type=F32)
        bound = jnp.sqrt(qn2 * kmax_ref[hh][0:1, :]) + 0.25
        qb.append(jnp.where(lane == BOUND_LANE, -bound, q.astype(F32)).astype(BF16))
        worst = jnp.maximum(worst, jnp.max(bound, axis=0, keepdims=True)[:, 0:1])
    safe = worst[0, 0] <= SAFE_BOUND_LOG2

    @pl.when(safe)
    def _():
        acc_ref[...] = jnp.zeros_like(acc_ref)

        skew = (lax.broadcasted_iota(I32, (tq, tk), 1) - lax.broadcasted_iota(I32, (tq, tk), 0))
        last = lp // tk - 1

        def logits(j, buf, r0=0):
            jj = jnp.minimum(j, last)
            for hh in range(nh):
                k, _ = kv(hh, jj)
                s = lax.dot_general(qb[hh][r0:], k, nt, preferred_element_type=F32)
                s_ref[buf, hh, r0:, :] = jnp.where(skew[r0:] <= i * tq - j * tk, s, NEG)

        def consume(j, buf, r0=0):
            jj = jnp.minimum(j, last)
            for hh in range(nh):
                _, v = kv(hh, jj)
                p = jnp.exp2(s_ref[buf, hh, r0:, :]).astype(BF16)
                acc_ref[hh, r0:, :] += jnp.dot(p, v, preferred_element_type=F32)

        def step(j, cur, nxt):
            logits(j + 1, nxt)
            consume(j, cur)

        def quad(r, carry):
            for u in range(4):
                step(4 * r + u, u % 2, 1 - u % 2)
            return carry

        n_u = n_full + 1
        logits(0, 0)
        quads = (n_u - 1) // 4
        lax.fori_loop(0, quads, quad, 0)
        j4 = 4 * quads
        pairs = (n_u - j4) // 2

        def pair(r, carry):
            for u in range(2):
                step(j4 + 2 * r + u, u, 1 - u)
            return carry

        lax.fori_loop(0, pairs, pair, 0)
        tail = j4 + 2 * pairs

        def finish(first_d):
            pending = lambda: consume(tail, 0)
            buf = 1
            for dd in range(first_d, n_diag):
                logits(n_full + dd, buf, dd * tk)
                pending()
                pending = lambda dd=dd, buf=buf: consume(n_full + dd, buf, dd * tk)
                buf = 1 - buf
            pending()

        @pl.when(tail == n_full)
        def _():
            finish(1)

        @pl.when(tail != n_full)
        def _():
            finish(2)

    @pl.when(jnp.logical_not(safe))
    def _():
        for hh in range(nh):
            q = qa_ref[0, hh]

            def tile(j, carry, masked, hh=hh, q=q):
                m, acc = carry
                k, v = kv(hh, j)
                s = lax.dot_general(q, k, nt, preferred_element_type=F32)
                if masked:
                    qpos = i * tq + lax.broadcasted_iota(I32, (tq, tk), 0)
                    kpos = j * tk + lax.broadcasted_iota(I32, (tq, tk), 1)
                    s = jnp.where(kpos <= qpos, s, NEG)
                m_new = jnp.maximum(m, jnp.max(s, axis=1, keepdims=True))
                acc = jnp.exp2(m - m_new) * acc + jnp.dot(
                    jnp.exp2(s - m_new).astype(BF16), v, preferred_element_type=F32)
                return m_new, acc

            carry = (jnp.full((tq, 1), NEG, F32), jnp.zeros((tq, LANES), F32))
            carry = lax.fori_loop(0, n_full, functools.partial(tile, masked=False), carry)
            for d in range(n_diag):
                carry = tile(n_full + d, carry, True)
            acc_ref[hh] = carry[1]

    for pr in range(nh // 2):
        a0 = acc_ref[2 * pr]
        a1 = acc_ref[2 * pr + 1]
        o0 = a0 / a0[:, HEAD_DIM:HEAD_DIM + 1]
        o1 = a1 / a1[:, HEAD_DIM:HEAD_DIM + 1]
        o_ref[:, pr * LANES:(pr + 1) * LANES] = jnp.where(
            low, o0, pltpu.roll(o1, HEAD_DIM, 1)).astype(BF16)


def _attention(qa, ka, va):
    batch, _, lp, _ = qa.shape
    tq, tk, nh = TQ_ATTN, TK_ATTN, HEADS_PER_STEP
    nq = lp // tq
    return pl.pallas_call(
        functools.partial(_attn_kernel, tq=tq, tk=tk),
        grid=(batch, N_HEADS // nh, nq),
        in_specs=[
            pl.BlockSpec((1, nh, tq, LANES), lambda b, h, i: (b, h, i, 0)),
            pl.BlockSpec((1, nh, lp, LANES), lambda b, h, i: (b, h, 0, 0)),
            pl.BlockSpec((1, nh, lp, LANES), lambda b, h, i: (b, h, 0, 0)),
        ],
        out_specs=pl.BlockSpec((tq, nh * HEAD_DIM), lambda b, h, i: (b * nq + i, h)),
        out_shape=jax.ShapeDtypeStruct((batch * lp, N_HEADS * HEAD_DIM), BF16),
        scratch_shapes=[pltpu.VMEM((nh, tq, LANES), F32), pltpu.VMEM((nh, 8, LANES), F32),
                        pltpu.VMEM((2, nh, tq, tk), F32)],
        compiler_params=_cparams(("parallel", "parallel", "arbitrary")),
        name="attn",
    )(qa, ka, va)


def _merge_kernel(b_ref, c_ref, u_ref, ch_ref, uh_ref, ga_ref, gb_ref, at_ref, h_ref,
                  cw_ref, wc_ref, wa_ref, wm_ref, g2_ref, wr2_ref, br_ref,
                  ho_ref, xp_ref, rt_ref, *, tiles_per_seq):
    tm, d = h_ref.shape
    i = pl.program_id(0)

    z = c_ref[...].astype(F32) * u_ref[...].astype(F32)
    hv = (i % tiles_per_seq != 0).astype(F32)
    zh = ch_ref[...].astype(F32) * uh_ref[...].astype(F32) * hv
    hr = zh.shape[0]
    zm1 = zh[hr - 1:hr, :]
    zm2 = zh[hr - 2:hr - 1, :]
    rowid = lax.broadcasted_iota(I32, (tm, d), 0)
    zp1 = jnp.where(rowid == 0, zm1, pltpu.roll(z, 1, 0))
    zp2 = jnp.where(rowid == 0, zm2, jnp.where(rowid == 1, zm1, pltpu.roll(z, 2, 0)))
    cw = cw_ref[...]
    yc = b_ref[...].astype(F32) * (zp2 * cw[0:1, :] + zp1 * cw[1:2, :] + z * cw[2:3, :])

    y_a = jnp.dot(yc.astype(BF16), wc_ref[...], preferred_element_type=F32)
    y_b = jnp.dot(at_ref[...], wa_ref[...], preferred_element_type=F32)
    mix = (jax.nn.sigmoid(ga_ref[...].astype(F32)) * y_a
           + jax.nn.sigmoid(gb_ref[...].astype(F32)) * y_b)
    hnew = h_ref[...] + jnp.dot(mix.astype(BF16), wm_ref[...], preferred_element_type=F32)
    ho_ref[...] = hnew

    r = lax.rsqrt(jnp.mean(hnew * hnew, axis=-1, keepdims=True) + EPS)
    xn = hnew * r * g2_ref[...]
    x_hi = xn.astype(BF16)
    xp_ref[...] = _pack_bf16_halves(x_hi)

    x_lo = (xn - x_hi.astype(F32)).astype(BF16)
    both =jnp.dot(x_hi, wr2_ref[...], preferred_element_type=F32)
    logits = (both[:, :LANES] + both[:, LANES:]
              + jnp.dot(x_lo, wr2_ref[:, :LANES], preferred_element_type=F32)) + br_ref[...]

    lanei = lax.broadcasted_iota(I32, (tm, LANES), 1)
    lane = lanei.astype(F32)
    big = jnp.float32(1e9)
    is_g = lanei < N_GROUPS
    is_e = (lanei >= N_GROUPS) & (lanei < ROUTE_LANES)
    gl = jnp.where(is_g, logits, NEG)
    gmax = jnp.max(gl, axis=1, keepdims=True)
    gidx = jnp.min(jnp.where(gl == gmax, lane, big), axis=1, keepdims=True)
    gsum = jnp.sum(jnp.where(is_g, jnp.exp(gl - gmax), 0.0), axis=1, keepdims=True)
    grp = ((lanei - N_GROUPS) >> 3).astype(F32)
    el = jnp.where(is_e & (grp == gidx), logits, NEG)
    emax = jnp.max(el, axis=1, keepdims=True)
    i1 = jnp.min(jnp.where(el == emax, lane, big), axis=1, keepdims=True)
    el2 = jnp.where(lane == i1, NEG, el)
    e2max = jnp.max(el2, axis=1, keepdims=True)
    i2 = jnp.min(jnp.where(el2 == e2max, lane, big), axis=1, keepdims=True)
    t = jnp.exp(e2max - emax)
    w1 = 1.0 / (gsum * (1.0 + t))
    w2 = w1 * t
    rt_ref[...] = jnp.where(lanei == 0, i1 - N_GROUPS,
                            jnp.where(lanei == 1, i2 - N_GROUPS,
                                      jnp.where(lanei == 2, w1,
                                                jnp.where(lanei == 3, w2, 0.0))))


def _merge(proj, attn, h, cw, wc, wa, wm, g2, wr2, br, lp):
    n, d = h.shape
    tm = TM_MERGE
    halo = 16
    hb = tm // halo
    row = lambda c: pl.BlockSpec((tm, d), lambda i: (i, c))
    halo_spec = lambda c: pl.BlockSpec((halo, d), lambda i: (jnp.maximum(i * hb - 1, 0), c))
    full = lambda r, c: pl.BlockSpec((r, c), lambda i: (0, 0))
    return pl.pallas_call(
        functools.partial(_merge_kernel, tiles_per_seq=lp // tm),
        grid=(n // tm,),
        in_specs=[
            row(0), row(1), row(2), halo_spec(1), halo_spec(2), row(6), row(7),
            pl.BlockSpec((tm, d), lambda i: (i, 0)),
            pl.BlockSpec((tm, d), lambda i: (i, 0)),
            full(8, d), full(d, d), full(d, d), full(d, d), full(1, d),
            full(d, 2 * LANES), full(1, LANES),
        ],
        out_specs=[
            pl.BlockSpec((tm, d), lambda i: (i, 0)),
            pl.BlockSpec((tm, d // 2), lambda i: (i, 0)),
            pl.BlockSpec((tm, LANES), lambda i: (i, 0)),
        ],
        out_shape=[
            jax.ShapeDtypeStruct((n, d), F32),
            jax.ShapeDtypeStruct((n, d // 2), I32),
            jax.ShapeDtypeStruct((n, LANES), F32),
        ],
        compiler_params=_cparams(("parallel",)),
        name="merge",
    )(proj, proj, proj, proj, proj, proj, proj, attn, h,
      cw, wc, wa, wm, g2, wr2, br)


def _pos_kernel(rt_ref, pos_ref, blk_ref, cnt_ref, run_ref, pst_ref, *, bm):
    ph = pl.program_id(0)
    c = pl.program_id(1)
    tp = rt_ref.shape[0]
    nbp = blk_ref.shape[0]
    lane = lax.broadcasted_iota(I32, (tp, LANES), 1).astype(F32)
    rt = rt_ref[...]
    sel0 = lane == rt[:, 0:1]
    sel1 = lane == rt[:, 1:2]
    both = sel0.astype(F32) + sel1.astype(F32)
    colsum = jnp.sum(both, axis=0, keepdims=True)

    @pl.when((ph == 0) & (c == 0))
    def _():
        cnt_ref[...] = jnp.zeros_like(cnt_ref)

    @pl.when(ph == 0)
    def _():
        cnt_ref[...] += colsum

    @pl.when((ph == 1) & (c == 0))
    def _():
        nblk = jnp.floor((cnt_ref[...] + (bm - 1)) * (1.0 / bm))
        r0 = lax.broadcasted_iota(I32, (LANES, LANES), 0)
        c0 = lax.broadcasted_iota(I32, (LANES, LANES), 1)
        upper = (r0 < c0).astype(BF16)
        nblk8 = jnp.broadcast_to(nblk, (8, LANES)).astype(BF16)
        pstb = jnp.dot(nblk8, upper, preferred_element_type=F32)[0:1, :]
        pst_ref[...] = pstb * bm
        run_ref[...] = jnp.zeros_like(run_ref)
        pend = pstb + nblk
        bidx = lax.broadcasted_iota(I32, (nbp, LANES), 0).astype(F32)
        l2 = lax.broadcasted_iota(I32, (nbp, LANES), 1)
        started = ((pend <= bidx) & (l2 < N_EXPERTS)).astype(F32)
        eid = jnp.minimum(jnp.sum(started, axis=1, keepdims=True), N_EXPERTS - 1.0)
        mine = l2.astype(F32) == eid
        cnt_e = jnp.sum(jnp.where(mine, cnt_ref[...], 0.0), axis=1, keepdims=True)
        first = jnp.sum(jnp.where(mine, pstb, 0.0), axis=1, keepdims=True)
        nvalid = jnp.clip(cnt_e - (bidx[:, 0:1] - first) * bm, 0.0, float(bm))
        blk_ref[...] = jnp.where(l2 == 0, eid, jnp.where(l2 == 1, nvalid, 0.0)).astype(I32)

    @pl.when(ph == 1)
    def _():
        r0 = lax.broadcasted_iota(I32, (tp, tp), 0)
        c0 = lax.broadcasted_iota(I32, (tp, tp), 1)
        lower = (c0 < r0).astype(BF16)
        before = jnp.dot(lower, both.astype(BF16), preferred_element_type=F32)
        base = before + run_ref[...] + pst_ref[...]
        d0 = jnp.sum(jnp.where(sel0, base, 0.0), axis=1, keepdims=True)
        d1 = jnp.sum(jnp.where(sel1, base, 0.0), axis=1, keepdims=True)
        both_d = jnp.where(lane == 0.0, d0, jnp.where(lane == 1.0, d1, 0.0))
        pos_ref[...] = both_d.T[0:8, :].astype(I32)
        run_ref[...] += colsum


def _positions(route, nbp):
    n = route.shape[0]
    tp = TP_POS
    return pl.pallas_call(
        functools.partial(_pos_kernel, bm=BM_EXPERT),
        grid=(2, n // tp),
        in_specs=[pl.BlockSpec((tp, LANES), lambda ph, c: (c, 0))],
        out_specs=[
            pl.BlockSpec((8, tp), lambda ph, c: (0, c * ph)),
            pl.BlockSpec((nbp, LANES), lambda ph, c: (0, 0)),
        ],
        out_shape=[
            jax.ShapeDtypeStruct((8, n), I32),
            jax.ShapeDtypeStruct((nbp, LANES), I32),
        ],
        scratch_shapes=[pltpu.VMEM((1, LANES), F32)] * 3,
        compiler_params=_cparams(("arbitrary", "arbitrary")),
        name="positions",
    )(route)


def _sc_mesh():
    return plsc.VectorSubcoreMesh(core_axis_name="c", subcore_axis_name="s",
                                  num_cores=SC_CORES, num_subcores=SC_SUBCORES)


def _sc_worker_base(rows_per_worker):
    return (lax.axis_index("s") * SC_CORES + lax.axis_index("c")) * rows_per_worker


def _dispatch(dest, xp, p_rows):
    n, half = xp.shape
    per_worker = n // SC_WORKERS
    chunk = SC_CHUNK
    assert n % (8 * SC_WORKERS) == 0 and per_worker % chunk == 0

    def body(xp_hbm, dest_hbm, xs_hbm, i0_v, i1_v, rows_v):
        base = _sc_worker_base(per_worker)

        @pl.loop(0, per_worker // chunk)
        def _(c):
            off = pl.multiple_of(base + c * chunk, 8)
            pltpu.sync_copy(dest_hbm.at[pl.ds(off, chunk)], i0_v)
            pltpu.sync_copy(dest_hbm.at[pl.ds(n + off, chunk)], i1_v)
            pltpu.sync_copy(xp_hbm.at[pl.ds(off, chunk)], rows_v)
            pltpu.sync_copy(rows_v, xs_hbm.at[i0_v])
            pltpu.sync_copy(rows_v, xs_hbm.at[i1_v])

    return pl.kernel(
        body, mesh=_sc_mesh(),
        out_type=jax.ShapeDtypeStruct((p_rows, half), xp.dtype),
        scratch_types=[pltpu.VMEM((chunk,), I32), pltpu.VMEM((chunk,), I32),
                       pltpu.VMEM((chunk, half), xp.dtype)],
        name="dispatch",
    )(xp, dest)


def _gather_rows(ys, dest):
    a_rows = dest.shape[0]
    d = ys.shape[1]
    per_worker = a_rows // SC_WORKERS
    chunk = SC_CHUNK
    assert a_rows % (8 * SC_WORKERS) == 0 and per_worker % chunk == 0

    def body(ys_hbm, dest_hbm, out_hbm, idx_v, rows_v):
        base = _sc_worker_base(per_worker)

        @pl.loop(0, per_worker // chunk)
        def _(c):
            off = pl.multiple_of(base + c * chunk, 8)
            pltpu.sync_copy(dest_hbm.at[pl.ds(off, chunk)], idx_v)
            pltpu.sync_copy(ys_hbm.at[idx_v], rows_v)
            pltpu.sync_copy(rows_v, out_hbm.at[pl.ds(off, chunk)])

    return pl.kernel(
        body, mesh=_sc_mesh(),
        out_type=jax.ShapeDtypeStruct((a_rows, d), ys.dtype),
        scratch_types=[pltpu.VMEM((chunk,), I32), pltpu.VMEM((chunk, d), ys.dtype)],
        name="gather_rows",
    )(ys, dest)


def _expert_kernel(eid_ref, nvalid_ref, xs_ref, wg_ref, wu_ref, wd_ref, ys_ref,
                   wgb_ref, wub_ref, wdb_ref):
    b = pl.program_id(0)
    nvalid = nvalid_ref[b]

    @pl.when((b == 0) | (eid_ref[b] != eid_ref[jnp.maximum(b - 1, 0)]))
    def _():
        wgb_ref[...] = wg_ref[...].astype(BF16)
        wub_ref[...] = wu_ref[...].astype(BF16)
        wdb_ref[...] = wd_ref[...].astype(BF16)

    @pl.when(nvalid != 0)
    def _():
        rows = lax.broadcasted_iota(I32, xs_ref.shape, 0)
        w = jnp.where(rows < nvalid, xs_ref[...], 0)
        half = w.shape[1]
        lo, hi = (v.astype(BF16) for v in _unpack_bf16_halves(w))

        def proj(w_ref):
            return (jnp.dot(lo, w_ref[:half, :], preferred_element_type=F32)
                    + jnp.dot(hi, w_ref[half:, :], preferred_element_type=F32))

        hg = proj(wgb_ref)
        hu = proj(wub_ref)
        act = (hg * jax.nn.sigmoid(hg) * hu).astype(BF16)
        y = jnp.dot(act, wdb_ref[...], preferred_element_type=F32)
        ys_ref[...] = _pack_bf16_halves(y.astype(BF16))

    @pl.when(nvalid == 0)
    def _():
        ys_ref[...] = jnp.zeros_like(ys_ref)


def _experts(eid, nvalid, xs, wg, wu, wd, layer):
    p, half = xs.shape
    d = 2 * half
    de = wg.shape[3]
    bm = BM_EXPERT
    return pl.pallas_call(
        _expert_kernel,
        grid_spec=pltpu.PrefetchScalarGridSpec(
            num_scalar_prefetch=2,
            grid=(p // bm,),
            in_specs=[
                pl.BlockSpec((bm, half), lambda b, e, v: (b, 0)),
                pl.BlockSpec((None, None, d, de), lambda b, e, v: (layer, e[b], 0, 0)),
                pl.BlockSpec((None, None, d, de), lambda b, e, v: (layer, e[b], 0, 0)),
                pl.BlockSpec((None, None, de, d), lambda b, e, v: (layer, e[b], 0, 0)),
            ],
            out_specs=pl.BlockSpec((bm, half), lambda b, e, v: (b, 0)),
            scratch_shapes=[pltpu.VMEM((d, de), BF16), pltpu.VMEM((d, de), BF16),
                            pltpu.VMEM((de, d), BF16)],
        ),
        out_shape=jax.ShapeDtypeStruct((p, half), I32),
        compiler_params=_cparams(("arbitrary",)),
        name="experts",
    )(eid, nvalid, xs, wg, wu, wd)


def _combine_kernel(h_ref, rt_ref, g0_ref, g1_ref, o_ref):
    rt = rt_ref[...]
    half = g0_ref.shape[1]
    lo0, hi0 = _unpack_bf16_halves(g0_ref[...])
    lo1, hi1 = _unpack_bf16_halves(g1_ref[...])
    w0, w1 = rt[:, 2:3], rt[:, 3:4]
    o_ref[:, :half] = h_ref[:, :half] + w0 * lo0 + w1 * lo1
    o_ref[:, half:] = h_ref[:, half:] + w0 * hi0 + w1 * hi1


def _combine(h, route, g):
    n, d = h.shape
    half = g.shape[1]
    tc = TC_COMBINE
    nt = n // tc
    return pl.pallas_call(
        _combine_kernel,
        grid=(nt,),
        in_specs=[pl.BlockSpec((tc, d), lambda i: (i, 0)),
                  pl.BlockSpec((tc, LANES), lambda i: (i, 0)),
                  pl.BlockSpec((tc, half), lambda i: (i, 0)),
                  pl.BlockSpec((tc, half), lambda i: (nt + i, 0))],
        out_specs=pl.BlockSpec((tc, d), lambda i: (i, 0)),
        out_shape=jax.ShapeDtypeStruct((n, d), F32),
        compiler_params=_cparams(("parallel",)),
        name="combine",
    )(h, route, g, g)


def _pad_lanes(v, width=LANES):
    return jnp.pad(v, [(0, 0)] * (v.ndim - 1) + [(0, width - v.shape[-1])])


def kernel(x, meta_tokens, norm1_g, w_in, b_forget, conv_w, w_conv_out, q_norm_g, k_norm_g,
           w_att_out, w_merge_out, norm2_g, w_router_group, b_router_group, w_router_expert,
           b_router_expert, w_exp_gate, w_exp_up, w_exp_down):
    batch, seq, d = x.shape
    depth = w_in.shape[0]
    length = seq + N_META
    lp = -(-length // SEQ_ALIGN) * SEQ_ALIGN
    n = batch * lp
    assert d == N_HEADS * HEAD_DIM == 1024
    assert lp % TQ_ATTN == 0 and lp % TM_MERGE == 0 and lp % TR_PREP == 0
    assert n % TM_INPROJ == 0 and n % TC_COMBINE == 0 and n % TP_POS == 0

    n_blocks = -(-(2 * n + N_EXPERTS * (BM_EXPERT - 1)) // BM_EXPERT)
    nbp = -(-n_blocks // 8) * 8
    p_rows = n_blocks * BM_EXPERT

    meta = jnp.broadcast_to(meta_tokens[None].astype(x.dtype), (batch, N_META, d))
    h = jnp.concatenate([meta, x, jnp.zeros((batch, lp - length, d), x.dtype)], axis=1)
    h = h.reshape(n, d)

    scale = HEAD_DIM ** -0.5
    for l in range(depth):
        w = w_in[l]
        w_all = jnp.concatenate([w[:, :6 * d], w[:, 6 * d + N_HEADS:]], axis=1).astype(BF16)
        w_f = _pad_lanes(w[:, 6 * d:6 * d + N_HEADS]).astype(BF16)
        proj, f = _inproj(h, norm1_g[l][None], w_all, w_f)

        bf = _pad_lanes(b_forget[l][None])
        gq2 = jnp.tile(q_norm_g[l], 2)[None] * (scale * LOG2E)
        gk2 = jnp.tile(k_norm_g[l], 2)[None]
        qa, ka, va = _prep(proj, f, bf, gq2, gk2, batch, lp)
        attn = _attention(qa, ka, va)

        w_r = _pad_lanes(jnp.concatenate([w_router_group[l], w_router_expert[l]], axis=1))
        wr_hi = w_r.astype(BF16)
        wr2 = jnp.concatenate([wr_hi, (w_r - wr_hi.astype(F32)).astype(BF16)], axis=1)
        b_r = _pad_lanes(jnp.concatenate([b_router_group[l], b_router_expert[l]])[None])
        cw = jnp.pad(conv_w[l], ((0, 8 - conv_w.shape[1]), (0, 0)))
        h_mid, xp, route = _merge(
            proj, attn, h, cw, w_conv_out[l].astype(BF16), w_att_out[l].astype(BF16),
            w_merge_out[l].astype(BF16), norm2_g[l][None], wr2, b_r, lp)

        pos, blk = _positions(route, nbp)
        dest = pos[0:2].reshape(-1)
        xs = _dispatch(dest, xp, p_rows)
        ys = _experts(blk[:n_blocks, 0], blk[:n_blocks, 1], xs,
                      w_exp_gate, w_exp_up, w_exp_down, l)
        h = _combine(h_mid, route, _gather_rows(ys, dest))

    return h.reshape(batch, lp, d)[:, N_META:length]
```

```python
import functools

import jax
import jax.numpy as jnp
from jax import lax
from jax.experimental import pallas as pl
from jax.experimental.pallas import tpu as pltpu
from jax.experimental.pallas import tpu_sc as plsc

F32 = jnp.float32
BF16 = jnp.bfloat16
I32 = jnp.int32

N_META = 16
N_HEADS = 16
HEAD_DIM = 64
N_GROUPS = 4
EXPERTS_PER_GROUP = 8
N_EXPERTS = N_GROUPS * EXPERTS_PER_GROUP
EPS = 1e-6
NEG = -1e30
LOG2E = 1.4426950408889634
BOUND_LANE = HEAD_DIM + 6
SAFE_BOUND_LOG2 = 50.0

LANES = 128
SEQ_ALIGN = 256
ROUTE_LANES = 4 + N_EXPERTS

TM_INPROJ = 1536
TN_INPROJ = 2048
TR_PREP = 768
CUMSUM_ROWS = 128
TQ_ATTN = 768
TK_ATTN = 256
HEADS_PER_STEP = 4
TM_MERGE = 768
TP_POS = 1536
BM_EXPERT = 512
TC_COMBINE = 1408
TC_COMBINE_OUT = 1024

SC_CORES = 2
SC_SUBCORES = 16
SC_WORKERS = SC_CORES * SC_SUBCORES
SC_CHUNK = 176

VMEM_LIMIT = 56 * 1024 * 1024


def _cparams(sem, vmem=VMEM_LIMIT):
    return pltpu.CompilerParams(dimension_semantics=sem, vmem_limit_bytes=vmem)


def _pack_bf16_halves(x):
    bits = lax.bitcast_convert_type(x.astype(F32), I32)
    half = x.shape[1] // 2
    return lax.shift_right_logical(bits[:, :half], 16) | (bits[:, half:] & jnp.int32(-65536))


def _unpack_bf16_halves(w):
    return (lax.bitcast_convert_type(w << 16, F32),
            lax.bitcast_convert_type(w & jnp.int32(-65536), F32))


def _inproj_kernel(x_ref, g_ref, w_ref, wf_ref, o_ref, f_ref, hn_ref):
    @pl.when(pl.program_id(1) == 0)
    def _():
        x = x_ref[...]
        r = lax.rsqrt(jnp.mean(x * x, axis=-1, keepdims=True) + EPS)
        hn = (x * r * g_ref[...]).astype(BF16)
        hn_ref[...] = hn
        f_ref[...] = jnp.dot(hn, wf_ref[...], preferred_element_type=F32)

    o_ref[...] = jnp.dot(hn_ref[...], w_ref[...], preferred_element_type=F32).astype(BF16)


def _inproj(h, g, w_all, w_f):
    n, d = h.shape
    cols = w_all.shape[1]
    tm, tn = TM_INPROJ, TN_INPROJ
    return pl.pallas_call(
        _inproj_kernel,
        grid=(n // tm, cols // tn),
        in_specs=[
            pl.BlockSpec((tm, d), lambda i, j: (i, 0)),
            pl.BlockSpec((1, d), lambda i, j: (0, 0)),
            pl.BlockSpec((d, tn), lambda i, j: (0, j)),
            pl.BlockSpec((d, LANES), lambda i, j: (0, 0)),
        ],
        out_specs=[
            pl.BlockSpec((tm, tn), lambda i, j: (i, j)),
            pl.BlockSpec((tm, LANES), lambda i, j: (i, 0)),
        ],
        out_shape=[
            jax.ShapeDtypeStruct((n, cols), BF16),
            jax.ShapeDtypeStruct((n, LANES), F32),
        ],
        scratch_shapes=[pltpu.VMEM((tm, d), BF16)],
        compiler_params=_cparams(("parallel", "arbitrary")),
        name="inproj",
    )(h, g, w_all, w_f)


def _split3(c):
    hi = c.astype(BF16).astype(F32)
    r1 = c - hi
    mid = r1.astype(BF16).astype(F32)
    lo = (r1 - mid).astype(BF16).astype(F32)
    return hi, mid, lo


def _prep_kernel(q_ref, k_ref, v_ref, f_ref, bf_ref, gq_ref, gk_ref, hsel_ref, hexp_ref,
                 selq_ref, selk_ref,
                 qa_ref, ka_ref, va_ref, carry_ref):
    tr = f_ref.shape[0]

    @pl.when(pl.program_id(1) == 0)
    def _():
        carry_ref[...] = jnp.zeros_like(carry_ref)

    f = f_ref[...] + bf_ref[...]
    lf = jnp.minimum(f, 0.0) - jnp.log1p(jnp.exp(-jnp.abs(f)))
    sub = min(tr, CUMSUM_ROWS)
    row = lax.broadcasted_iota(I32, (sub, sub), 0)
    col = lax.broadcasted_iota(I32, (sub, sub), 1)
    tril = (col <= row).astype(F32)
    carry = carry_ref[...]
    parts = []
    for s0 in range(0, tr, sub):
        part = jnp.dot(tril, lf[s0:s0 + sub], precision=lax.Precision.HIGHEST,
                       preferred_element_type=F32) + carry
        carry = part[sub - 1:sub, :]
        parts.append(part)
    c = jnp.concatenate(parts, axis=0)
    carry_ref[...] = carry
    c_hi, c_mid, c_lo = _split3(c * LOG2E)

    lane = lax.broadcasted_iota(I32, (tr, LANES), 1)
    low = lane < HEAD_DIM
    ones_v = (lane == HEAD_DIM).astype(F32)

    heads = lane < N_HEADS
    c3 = (jnp.where(heads, c_hi, 0.0)
          + pltpu.roll(jnp.where(heads, c_mid, 0.0), N_HEADS, 1)
          + pltpu.roll(jnp.where(heads, c_lo, 0.0), 2 * N_HEADS, 1)
          + (lane == LANES - 1).astype(F32)).astype(BF16)

    def inv_rms(x):
        ss = jnp.dot((x * x).astype(BF16), hsel_ref[...], preferred_element_type=F32)
        r = lax.rsqrt(ss * (1.0 / HEAD_DIM) + EPS)
        r_hi = r.astype(BF16)
        r_lo = (r - r_hi.astype(F32)).astype(BF16)
        return jnp.dot(jnp.concatenate([r_hi, r_lo], axis=1), hexp_ref[...],
                       preferred_element_type=F32)

    rq = inv_rms(q_ref[...].astype(F32))
    rk = inv_rms(k_ref[...].astype(F32))

    for p in range(N_HEADS // 2):
        sl = slice(p * LANES, (p + 1) * LANES)
        e, o = 2 * p, 2 * p + 1
        qn = q_ref[:, sl].astype(F32) * rq[:, sl] * gq_ref[...]
        kn = k_ref[:, sl].astype(F32) * rk[:, sl] * gk_ref[...]
        vv = v_ref[:, sl].astype(F32)
        pc = slice(2 * p * LANES, (2 * p + 2) * LANES)
        ext_q = jnp.dot(c3, selq_ref[:, pc], preferred_element_type=F32)
        ext_k = jnp.dot(c3, selk_ref[:, pc], preferred_element_type=F32)
        for h in (e, o):
            if h == e:
                qh, kh, vh = qn, kn, vv
            else:
                qh = pltpu.roll(qn, HEAD_DIM, 1)
                kh = pltpu.roll(kn, HEAD_DIM, 1)
                vh = pltpu.roll(vv, HEAD_DIM, 1)
            hs = slice((h - e) * LANES, (h - e + 1) * LANES)
            qa_ref[0, h] = jnp.where(low, qh, ext_q[:, hs]).astype(BF16)
            ka_ref[0, h] = jnp.where(low, kh, ext_k[:, hs]).astype(BF16)
            va_ref[0, h] = jnp.where(low, vh, ones_v).astype(BF16)


def _prep_constants():
    feat = jnp.arange(N_HEADS * HEAD_DIM)
    hsel = (feat[:, None] // HEAD_DIM == jnp.arange(LANES)[None, :]).astype(BF16)
    hexp = jnp.concatenate([hsel.T, hsel.T], axis=0)
    rows = jnp.arange(LANES)
    part, src = rows // N_HEADS, rows % N_HEADS
    cols = jnp.arange(N_HEADS * LANES)
    head, off = cols // LANES, cols % LANES - HEAD_DIM
    is_head = (src[:, None] == head[None, :]) & (part[:, None] < 3)
    const_row = (rows == LANES - 1)[:, None]
    sel_q = (is_head & (off[None, :] == part[:, None])).astype(F32) \
        + (const_row & (off[None, :] >= 3) & (off[None, :] < 6)).astype(F32)
    sel_k = -(is_head & (off[None, :] == part[:, None] + 3)).astype(F32) \
        + (const_row & (((off[None, :] >= 0) & (off[None, :] < 3))
                        | (off[None, :] == BOUND_LANE - HEAD_DIM))).astype(F32)
    return hsel, hexp, sel_q.astype(BF16), sel_k.astype(BF16)


def _prep(proj, f, bf, gq2, gk2, batch, lp):
    tr = TR_PREP
    nr = lp // tr
    hsel, hexp, sel_q, sel_k = _prep_constants()
    aug = jax.ShapeDtypeStruct((batch, N_HEADS, lp, LANES), BF16)
    aug_spec = pl.BlockSpec((1, N_HEADS, tr, LANES), lambda b, r: (b, 0, r, 0))
    vec = pl.BlockSpec((1, LANES), lambda b, r: (0, 0))
    full = lambda a: pl.BlockSpec(a.shape, lambda b, r: (0, 0))
    return pl.pallas_call(
        _prep_kernel,
        grid=(batch, nr),
        in_specs=[
            pl.BlockSpec((tr, 1024), lambda b, r: (b * nr + r, 3)),
            pl.BlockSpec((tr, 1024), lambda b, r: (b * nr + r, 4)),
            pl.BlockSpec((tr, 1024), lambda b, r: (b * nr + r, 5)),
            pl.BlockSpec((tr, LANES), lambda b, r: (b * nr + r, 0)),
            vec, vec, vec, full(hsel), full(hexp), full(sel_q), full(sel_k),
        ],
        out_specs=[aug_spec, aug_spec, aug_spec],
        out_shape=[aug, aug, aug],
        scratch_shapes=[pltpu.VMEM((1, LANES), F32)],
        compiler_params=_cparams(("parallel", "arbitrary")),
        name="prep",
    )(proj, proj, proj, f, bf, gq2, gk2, hsel, hexp, sel_q, sel_k)


def _attn_kernel(qa_ref, ka_ref, va_ref, o_ref, acc_ref, kmax_ref, s_ref, *, tq, tk):
    i = pl.program_id(2)
    nh = qa_ref.shape[1]
    lp = ka_ref.shape[2]
    assert tq % tk == 0 and nh % 2 == 0
    n_diag = tq // tk
    n_full = i * n_diag
    lane = lax.broadcasted_iota(I32, (tq, LANES), 1)
    low = lane < HEAD_DIM
    nt = (((1,), (1,)), ((), ()))

    @pl.when(i == 0)
    def _():
        feat = (lax.broadcasted_iota(I32, (LANES, LANES), 0) < HEAD_DIM).astype(BF16)

        def chunk(r, best):
            out = []
            for hh in range(nh):
                kk = ka_ref[0, hh, pl.ds(pl.multiple_of(r * tk, tk), tk), :]
                n2 = jnp.dot(kk * kk, feat, preferred_element_type=F32)
                out.append(jnp.maximum(best[hh], jnp.max(n2.reshape(tk // 8, 8, LANES), axis=0)))
            return tuple(out)

        best = lax.fori_loop(0, lp // tk, chunk, (jnp.zeros((8, LANES), F32),) * nh)
        for hh in range(nh):
            kmax_ref[hh] = jnp.broadcast_to(1.01 * jnp.max(best[hh], axis=0, keepdims=True),
                                            (8, LANES))

    def kv(hh, j):
        start = pl.multiple_of(j * tk, tk)
        return (ka_ref[0, hh, pl.ds(start, tk), :], va_ref[0, hh, pl.ds(start, tk), :])

    qb = []
    worst = jnp.zeros((1, 1), F32)
    feat_q = (lax.broadcasted_iota(I32, (LANES, LANES), 0) < HEAD_DIM).astype(BF16)
    for hh in range(nh):
        q = qa_ref[0, hh]
        qn2 = 1.01 * jnp.dot(q * q, feat_q, preferred_element_type=F32)
        bound = jnp.sqrt(qn2 * kmax_ref[hh][0:1, :]) + 0.25
        qb.append(jnp.where(lane == BOUND_LANE, -bound, q.astype(F32)).astype(BF16))
        worst = jnp.maximum(worst, jnp.max(bound, axis=0, keepdims=True)[:, 0:1])
    safe = worst[0, 0] <= SAFE_BOUND_LOG2

    @pl.when(safe)
    def _():
        acc_ref[...] = jnp.zeros_like(acc_ref)

        skew = (lax.broadcasted_iota(I32, (tq, tk), 1) - lax.broadcasted_iota(I32, (tq, tk), 0))
        last = lp // tk - 1

        def logits(j, buf, r0=0):
            jj = jnp.minimum(j, last)
            for hh in range(nh):
                k, _ = kv(hh, jj)
                s = lax.dot_general(qb[hh][r0:], k, nt, preferred_element_type=F32)
                s_ref[buf, hh, r0:, :] = jnp.where(skew[r0:] <= i * tq - j * tk, s, NEG)

        def consume(j, buf, r0=0):
            jj = jnp.minimum(j, last)
            for hh in range(nh):
                _, v = kv(hh, jj)
                p = jnp.exp2(s_ref[buf, hh, r0:, :]).astype(BF16)
                acc_ref[hh, r0:, :] += jnp.dot(p, v, preferred_element_type=F32)

        def step(j, cur, nxt):
            logits(j + 1, nxt)
            consume(j, cur)

        def quad(r, carry):
            for u in range(4):
                step(4 * r + u, u % 2, 1 - u % 2)
            return carry

        n_u = n_full + 1
        logits(0, 0)
        quads = (n_u - 1) // 4
        lax.fori_loop(0, quads, quad, 0)
        j4 = 4 * quads
        pairs = (n_u - j4) // 2

        def pair(r, carry):
            for u in range(2):
                step(j4 + 2 * r + u, u, 1 - u)
            return carry

        lax.fori_loop(0, pairs, pair, 0)
        tail = j4 + 2 * pairs

        def finish(first_d):
            pending = lambda: consume(tail, 0)
            buf = 1
            for dd in range(first_d, n_diag):
                logits(n_full + dd, buf, dd * tk)
                pending()
                pending = lambda dd=dd, buf=buf: consume(n_full + dd, buf, dd * tk)
                buf = 1 - buf
            pending()

        @pl.when(tail == n_full)
        def _():
            finish(1)

        @pl.when(tail != n_full)
        def _():
            finish(2)

    @pl.when(jnp.logical_not(safe))
    def _():
        for hh in range(nh):
            q = qa_ref[0, hh]

            def tile(j, carry, masked, hh=hh, q=q):
                m, acc = carry
                k, v = kv(hh, j)
                s = lax.dot_general(q, k, nt, preferred_element_type=F32)
                if masked:
                    qpos = i * tq + lax.broadcasted_iota(I32, (tq, tk), 0)
                    kpos = j * tk + lax.broadcasted_iota(I32, (tq, tk), 1)
                    s = jnp.where(kpos <= qpos, s, NEG)
                m_new = jnp.maximum(m, jnp.max(s, axis=1, keepdims=True))
                acc = jnp.exp2(m - m_new) * acc + jnp.dot(
                    jnp.exp2(s - m_new).astype(BF16), v, preferred_element_type=F32)
                return m_new, acc

            carry = (jnp.full((tq, 1), NEG, F32), jnp.zeros((tq, LANES), F32))
            carry = lax.fori_loop(0, n_full, functools.partial(tile, masked=False), carry)
            for d in range(n_diag):
                carry = tile(n_full + d, carry, True)
            acc_ref[hh] = carry[1]

    for pr in range(nh // 2):
        a0 = acc_ref[2 * pr]
        a1 = acc_ref[2 * pr + 1]
        o0 = a0 / a0[:, HEAD_DIM:HEAD_DIM + 1]
        o1 = a1 / a1[:, HEAD_DIM:HEAD_DIM + 1]
        o_ref[:, pr * LANES:(pr + 1) * LANES] = jnp.where(
            low, o0, pltpu.roll(o1, HEAD_DIM, 1)).astype(BF16)


def _attention(qa, ka, va):
    batch, _, lp, _ = qa.shape
    tq, tk, nh = TQ_ATTN, TK_ATTN, HEADS_PER_STEP
    nq = lp // tq
    return pl.pallas_call(
        functools.partial(_attn_kernel, tq=tq, tk=tk),
        grid=(batch, N_HEADS // nh, nq),
        in_specs=[
            pl.BlockSpec((1, nh, tq, LANES), lambda b, h, i: (b, h, i, 0)),
            pl.BlockSpec((1, nh, lp, LANES), lambda b, h, i: (b, h, 0, 0)),
            pl.BlockSpec((1, nh, lp, LANES), lambda b, h, i: (b, h, 0, 0)),
        ],
        out_specs=pl.BlockSpec((tq, nh * HEAD_DIM), lambda b, h, i: (b * nq + i, h)),
        out_shape=jax.ShapeDtypeStruct((batch * lp, N_HEADS * HEAD_DIM), BF16),
        scratch_shapes=[pltpu.VMEM((nh, tq, LANES), F32), pltpu.VMEM((nh, 8, LANES), F32),
                        pltpu.VMEM((2, nh, tq, tk), F32)],
        compiler_params=_cparams(("parallel", "parallel", "arbitrary")),
        name="attn",
    )(qa, ka, va)


def _merge_kernel(b_ref, c_ref, u_ref, ch_ref, uh_ref, ga_ref, gb_ref, at_ref, h_ref,
                  cw_ref, wc_ref, wa_ref, wm_ref, g2_ref, wr2_ref, br_ref,
                  ho_ref, xp_ref, rt_ref, *, tiles_per_seq):
    tm, d = h_ref.shape
    i = pl.program_id(0)

    z = c_ref[...].astype(F32) * u_ref[...].astype(F32)
    hv = (i % tiles_per_seq != 0).astype(F32)
    zh = ch_ref[...].astype(F32) * uh_ref[...].astype(F32) * hv
    hr = zh.shape[0]
    zm1 = zh[hr - 1:hr, :]
    zm2 = zh[hr - 2:hr - 1, :]
    rowid = lax.broadcasted_iota(I32, (tm, d), 0)
    zp1 = jnp.where(rowid == 0, zm1, pltpu.roll(z, 1, 0))
    zp2 = jnp.where(rowid == 0, zm2, jnp.where(rowid == 1, zm1, pltpu.roll(z, 2, 0)))
    cw = cw_ref[...]
    yc = b_ref[...].astype(F32) * (zp2 * cw[0:1, :] + zp1 * cw[1:2, :] + z * cw[2:3, :])

    y_a = jnp.dot(yc.astype(BF16), wc_ref[...], preferred_element_type=F32)
    y_b = jnp.dot(at_ref[...], wa_ref[...], preferred_element_type=F32)
    mix = (jax.nn.sigmoid(ga_ref[...].astype(F32)) * y_a
           + jax.nn.sigmoid(gb_ref[...].astype(F32)) * y_b)
    hnew = h_ref[...] + jnp.dot(mix.astype(BF16), wm_ref[...], preferred_element_type=F32)
    ho_ref[...] = hnew

    r = lax.rsqrt(jnp.mean(hnew * hnew, axis=-1, keepdims=True) + EPS)
    xn = hnew * r * g2_ref[...]
    x_hi = xn.astype(BF16)
    xp_ref[...] = _pack_bf16_halves(x_hi)

    x_lo = (xn - x_hi.astype(F32)).astype(BF16)
    both =jnp.dot(x_hi, wr2_ref[...], preferred_element_type=F32)
    logits = (both[:, :LANES] + both[:, LANES:]
              + jnp.dot(x_lo, wr2_ref[:, :LANES], preferred_element_type=F32)) + br_ref[...]

    lanei = lax.broadcasted_iota(I32, (tm, LANES), 1)
    lane = lanei.astype(F32)
    big = jnp.float32(1e9)
    is_g = lanei < N_GROUPS
    is_e = (lanei >= N_GROUPS) & (lanei < ROUTE_LANES)
    gl = jnp.where(is_g, logits, NEG)
    gmax = jnp.max(gl, axis=1, keepdims=True)
    gidx = jnp.min(jnp.where(gl == gmax, lane, big), axis=1, keepdims=True)
    gsum = jnp.sum(jnp.where(is_g, jnp.exp(gl - gmax), 0.0), axis=1, keepdims=True)
    grp = ((lanei - N_GROUPS) >> 3).astype(F32)
    el = jnp.where(is_e & (grp == gidx), logits, NEG)
    emax = jnp.max(el, axis=1, keepdims=True)
    i1 = jnp.min(jnp.where(el == emax, lane, big), axis=1, keepdims=True)
    el2 = jnp.where(lane == i1, NEG, el)
    e2max = jnp.max(el2, axis=1, keepdims=True)
    i2 = jnp.min(jnp.where(el2 == e2max, lane, big), axis=1, keepdims=True)
    t = jnp.exp(e2max - emax)
    w1 = 1.0 / (gsum * (1.0 + t))
    w2 = w1 * t
    rt_ref[...] = jnp.where(lanei == 0, i1 - N_GROUPS,
                            jnp.where(lanei == 1, i2 - N_GROUPS,
                                      jnp.where(lanei == 2, w1,
                                                jnp.where(lanei == 3, w2, 0.0))))


def _merge(proj, attn, h, cw, wc, wa, wm, g2, wr2, br, lp):
    n, d = h.shape
    tm = TM_MERGE
    halo = 16
    hb = tm // halo
    row = lambda c: pl.BlockSpec((tm, d), lambda i: (i, c))
    halo_spec = lambda c: pl.BlockSpec((halo, d), lambda i: (jnp.maximum(i * hb - 1, 0), c))
    full = lambda r, c: pl.BlockSpec((r, c), lambda i: (0, 0))
    return pl.pallas_call(
        functools.partial(_merge_kernel, tiles_per_seq=lp // tm),
        grid=(n // tm,),
        in_specs=[
            row(0), row(1), row(2), halo_spec(1), halo_spec(2), row(6), row(7),
            pl.BlockSpec((tm, d), lambda i: (i, 0)),
            pl.BlockSpec((tm, d), lambda i: (i, 0)),
            full(8, d), full(d, d), full(d, d), full(d, d), full(1, d),
            full(d, 2 * LANES), full(1, LANES),
        ],
        out_specs=[
            pl.BlockSpec((tm, d), lambda i: (i, 0)),
            pl.BlockSpec((tm, d // 2), lambda i: (i, 0)),
            pl.BlockSpec((tm, LANES), lambda i: (i, 0)),
        ],
        out_shape=[
            jax.ShapeDtypeStruct((n, d), F32),
            jax.ShapeDtypeStruct((n, d // 2), I32),
            jax.ShapeDtypeStruct((n, LANES), F32),
        ],
        compiler_params=_cparams(("parallel",)),
        name="merge",
    )(proj, proj, proj, proj, proj, proj, proj, attn, h,
      cw, wc, wa, wm, g2, wr2, br)


def _pos_kernel(rt_ref, pos_ref, blk_ref, cnt_ref, run_ref, pst_ref, *, bm):
    ph = pl.program_id(0)
    c = pl.program_id(1)
    tp = rt_ref.shape[0]
    nbp = blk_ref.shape[0]
    lane = lax.broadcasted_iota(I32, (tp, LANES), 1).astype(F32)
    rt = rt_ref[...]
    sel0 = lane == rt[:, 0:1]
    sel1 = lane == rt[:, 1:2]
    both = sel0.astype(F32) + sel1.astype(F32)
    colsum = jnp.sum(both, axis=0, keepdims=True)

    @pl.when((ph == 0) & (c == 0))
    def _():
        cnt_ref[...] = jnp.zeros_like(cnt_ref)

    @pl.when(ph == 0)
    def _():
        cnt_ref[...] += colsum

    @pl.when((ph == 1) & (c == 0))
    def _():
        nblk = jnp.floor((cnt_ref[...] + (bm - 1)) * (1.0 / bm))
        r0 = lax.broadcasted_iota(I32, (LANES, LANES), 0)
        c0 = lax.broadcasted_iota(I32, (LANES, LANES), 1)
        upper = (r0 < c0).astype(BF16)
        nblk8 = jnp.broadcast_to(nblk, (8, LANES)).astype(BF16)
        pstb = jnp.dot(nblk8, upper, preferred_element_type=F32)[0:1, :]
        pst_ref[...] = pstb * bm
        run_ref[...] = jnp.zeros_like(run_ref)
        pend = pstb + nblk
        bidx = lax.broadcasted_iota(I32, (nbp, LANES), 0).astype(F32)
        l2 = lax.broadcasted_iota(I32, (nbp, LANES), 1)
        started = ((pend <= bidx) & (l2 < N_EXPERTS)).astype(F32)
        eid = jnp.minimum(jnp.sum(started, axis=1, keepdims=True), N_EXPERTS - 1.0)
        mine = l2.astype(F32) == eid
        cnt_e = jnp.sum(jnp.where(mine, cnt_ref[...], 0.0), axis=1, keepdims=True)
        first = jnp.sum(jnp.where(mine, pstb, 0.0), axis=1, keepdims=True)
        nvalid = jnp.clip(cnt_e - (bidx[:, 0:1] - first) * bm, 0.0, float(bm))
        blk_ref[...] = jnp.where(l2 == 0, eid, jnp.where(l2 == 1, nvalid, 0.0)).astype(I32)

    @pl.when(ph == 1)
    def _():
        r0 = lax.broadcasted_iota(I32, (tp, tp), 0)
        c0 = lax.broadcasted_iota(I32, (tp, tp), 1)
        lower = (c0 < r0).astype(BF16)
        before = jnp.dot(lower, both.astype(BF16), preferred_element_type=F32)
        base = before + run_ref[...] + pst_ref[...]
        d0 = jnp.sum(jnp.where(sel0, base, 0.0), axis=1, keepdims=True)
        d1 = jnp.sum(jnp.where(sel1, base, 0.0), axis=1, keepdims=True)
        both_d = jnp.where(lane == 0.0, d0, jnp.where(lane == 1.0, d1, 0.0))
        pos_ref[...] = both_d.T[0:8, :].astype(I32)
        run_ref[...] += colsum


def _positions(route, nbp):
    n = route.shape[0]
    tp = TP_POS
    return pl.pallas_call(
        functools.partial(_pos_kernel, bm=BM_EXPERT),
        grid=(2, n // tp),
        in_specs=[pl.BlockSpec((tp, LANES), lambda ph, c: (c, 0))],
        out_specs=[
            pl.BlockSpec((8, tp), lambda ph, c: (0, c * ph)),
            pl.BlockSpec((nbp, LANES), lambda ph, c: (0, 0)),
        ],
        out_shape=[
            jax.ShapeDtypeStruct((8, n), I32),
            jax.ShapeDtypeStruct((nbp, LANES), I32),
        ],
        scratch_shapes=[pltpu.VMEM((1, LANES), F32)] * 3,
        compiler_params=_cparams(("arbitrary", "arbitrary")),
        name="positions",
    )(route)


def _sc_mesh():
    return plsc.VectorSubcoreMesh(core_axis_name="c", subcore_axis_name="s",
                                  num_cores=SC_CORES, num_subcores=SC_SUBCORES)


def _sc_worker_base(rows_per_worker):
    return (lax.axis_index("s") * SC_CORES + lax.axis_index("c")) * rows_per_worker


def _dispatch(dest, xp, p_rows):
    n, half = xp.shape
    per_worker = n // SC_WORKERS
    chunk = SC_CHUNK
    assert n % (8 * SC_WORKERS) == 0 and per_worker % chunk == 0

    def body(xp_hbm, dest_hbm, xs_hbm, i0_v, i1_v, rows_v):
        base = _sc_worker_base(per_worker)

        @pl.loop(0, per_worker // chunk)
        def _(c):
            off = pl.multiple_of(base + c * chunk, 8)
            pltpu.sync_copy(dest_hbm.at[pl.ds(off, chunk)], i0_v)
            pltpu.sync_copy(dest_hbm.at[pl.ds(n + off, chunk)], i1_v)
            pltpu.sync_copy(xp_hbm.at[pl.ds(off, chunk)], rows_v)
            pltpu.sync_copy(rows_v, xs_hbm.at[i0_v])
            pltpu.sync_copy(rows_v, xs_hbm.at[i1_v])

    return pl.kernel(
        body, mesh=_sc_mesh(),
        out_type=jax.ShapeDtypeStruct((p_rows, half), xp.dtype),
        scratch_types=[pltpu.VMEM((chunk,), I32), pltpu.VMEM((chunk,), I32),
                       pltpu.VMEM((chunk, half), xp.dtype)],
        name="dispatch",
    )(xp, dest)


def _gather_rows(ys, dest):
    a_rows = dest.shape[0]
    d = ys.shape[1]
    per_worker = a_rows // SC_WORKERS
    chunk = SC_CHUNK
    assert a_rows % (8 * SC_WORKERS) == 0 and per_worker % chunk == 0

    def body(ys_hbm, dest_hbm, out_hbm, idx_v, rows_v):
        base = _sc_worker_base(per_worker)

        @pl.loop(0, per_worker // chunk)
        def _(c):
            off = pl.multiple_of(base + c * chunk, 8)
            pltpu.sync_copy(dest_hbm.at[pl.ds(off, chunk)], idx_v)
            pltpu.sync_copy(ys_hbm.at[idx_v], rows_v)
            pltpu.sync_copy(rows_v, out_hbm.at[pl.ds(off, chunk)])

    return pl.kernel(
        body, mesh=_sc_mesh(),
        out_type=jax.ShapeDtypeStruct((a_rows, d), ys.dtype),
        scratch_types=[pltpu.VMEM((chunk,), I32), pltpu.VMEM((chunk, d), ys.dtype)],
        name="gather_rows",
    )(ys, dest)


def _expert_kernel(eid_ref, nvalid_ref, xs_ref, wg_ref, wu_ref, wd_ref, ys_ref,
                   wgb_ref, wub_ref, wdb_ref):
    b = pl.program_id(0)
    nvalid = nvalid_ref[b]

    @pl.when((b == 0) | (eid_ref[b] != eid_ref[jnp.maximum(b - 1, 0)]))
    def _():
        wgb_ref[...] = wg_ref[...].astype(BF16)
        wub_ref[...] = wu_ref[...].astype(BF16)
        wdb_ref[...] = wd_ref[...].astype(BF16)

    @pl.when(nvalid != 0)
    def _():
        rows = lax.broadcasted_iota(I32, xs_ref.shape, 0)
        w = jnp.where(rows < nvalid, xs_ref[...], 0)
        half = w.shape[1]
        lo, hi = (v.astype(BF16) for v in _unpack_bf16_halves(w))

        def proj(w_ref):
            return (jnp.dot(lo, w_ref[:half, :], preferred_element_type=F32)
                    + jnp.dot(hi, w_ref[half:, :], preferred_element_type=F32))

        hg = proj(wgb_ref)
        hu = proj(wub_ref)
        act = (hg * jax.nn.sigmoid(hg) * hu).astype(BF16)
        y = jnp.dot(act, wdb_ref[...], preferred_element_type=F32)
        ys_ref[...] = _pack_bf16_halves(y.astype(BF16))

    @pl.when(nvalid == 0)
    def _():
        ys_ref[...] = jnp.zeros_like(ys_ref)


def _experts(eid, nvalid, xs, wg, wu, wd, layer):
    p, half = xs.shape
    d = 2 * half
    de = wg.shape[3]
    bm = BM_EXPERT
    return pl.pallas_call(
        _expert_kernel,
        grid_spec=pltpu.PrefetchScalarGridSpec(
            num_scalar_prefetch=2,
            grid=(p // bm,),
            in_specs=[
                pl.BlockSpec((bm, half), lambda b, e, v: (b, 0)),
                pl.BlockSpec((None, None, d, de), lambda b, e, v: (layer, e[b], 0, 0)),
                pl.BlockSpec((None, None, d, de), lambda b, e, v: (layer, e[b], 0, 0)),
                pl.BlockSpec((None, None, de, d), lambda b, e, v: (layer, e[b], 0, 0)),
            ],
            out_specs=pl.BlockSpec((bm, half), lambda b, e, v: (b, 0)),
            scratch_shapes=[pltpu.VMEM((d, de), BF16), pltpu.VMEM((d, de), BF16),
                            pltpu.VMEM((de, d), BF16)],
        ),
        out_shape=jax.ShapeDtypeStruct((p, half), I32),
        compiler_params=_cparams(("arbitrary",)),
        name="experts",
    )(eid, nvalid, xs, wg, wu, wd)


def _combine_kernel(h_ref, rt_ref, g0_ref, g1_ref, o_ref):
    rt = rt_ref[...]
    half = g0_ref.shape[1]
    lo0, hi0 = _unpack_bf16_halves(g0_ref[...])
    lo1, hi1 = _unpack_bf16_halves(g1_ref[...])
    w0, w1 = rt[:, 2:3], rt[:, 3:4]
    o_ref[:, :half] = h_ref[:, :half] + w0 * lo0 + w1 * lo1
    o_ref[:, half:] = h_ref[:, half:] + w0 * hi0 + w1 * hi1


def _combine(h, route, g):
    n, d = h.shape
    half = g.shape[1]
    tc = TC_COMBINE
    nt = n // tc
    return pl.pallas_call(
        _combine_kernel,
        grid=(nt,),
        in_specs=[pl.BlockSpec((tc, d), lambda i: (i, 0)),
                  pl.BlockSpec((tc, LANES), lambda i: (i, 0)),
                  pl.BlockSpec((tc, half), lambda i: (i, 0)),
                  pl.BlockSpec((tc, half), lambda i: (nt + i, 0))],
        out_specs=pl.BlockSpec((tc, d), lambda i: (i, 0)),
        out_shape=jax.ShapeDtypeStruct((n, d), F32),
        compiler_params=_cparams(("parallel",)),
        name="combine",
    )(h, route, g, g)


def _combine_output(h, route, g, batch, lp, seq):
    n, d = h.shape
    half = g.shape[1]
    tc = TC_COMBINE_OUT
    per_seq = seq // tc
    assert seq % tc == 0 and N_META % 8 == 0 and lp % 8 == 0

    def rows(offset):
        return lambda b, i: (pl.multiple_of(offset + b * lp + N_META + i * tc, 8), 0)

    def spec(width, offset=0):
        return pl.BlockSpec((pl.Element(tc), pl.Element(width)), rows(offset))

    return pl.pallas_call(
        _combine_kernel,
        grid=(batch, per_seq),
        in_specs=[spec(d), spec(LANES), spec(half), spec(half, n)],
        out_specs=pl.BlockSpec((tc, d), lambda b, i: (b * per_seq + i, 0)),
        out_shape=jax.ShapeDtypeStruct((batch * seq, d), F32),
        compiler_params=_cparams(("parallel", "parallel")),
        name="combine_output",
    )(h, route, g, g)


def _pad_lanes(v, width=LANES):
    return jnp.pad(v, [(0, 0)] * (v.ndim - 1) + [(0, width - v.shape[-1])])


def kernel(x, meta_tokens, norm1_g, w_in, b_forget, conv_w, w_conv_out, q_norm_g, k_norm_g,
           w_att_out, w_merge_out, norm2_g, w_router_group, b_router_group, w_router_expert,
           b_router_expert, w_exp_gate, w_exp_up, w_exp_down):
    batch, seq, d = x.shape
    depth = w_in.shape[0]
    length = seq + N_META
    lp = -(-length // SEQ_ALIGN) * SEQ_ALIGN
    n = batch * lp
    assert d == N_HEADS * HEAD_DIM == 1024
    assert lp % TQ_ATTN == 0 and lp % TM_MERGE == 0 and lp % TR_PREP == 0
    assert n % TM_INPROJ == 0 and n % TC_COMBINE == 0 and n % TP_POS == 0

    n_blocks = -(-(2 * n + N_EXPERTS * (BM_EXPERT - 1)) // BM_EXPERT)
    nbp = -(-n_blocks // 8) * 8
    p_rows = n_blocks * BM_EXPERT

    meta = jnp.broadcast_to(meta_tokens[None].astype(x.dtype), (batch, N_META, d))
    h = jnp.concatenate([meta, x, jnp.zeros((batch, lp - length, d), x.dtype)], axis=1)
    h = h.reshape(n, d)

    scale = HEAD_DIM ** -0.5
    for l in range(depth):
        w = w_in[l]
        w_all = jnp.concatenate([w[:, :6 * d], w[:, 6 * d + N_HEADS:]], axis=1).astype(BF16)
        w_f = _pad_lanes(w[:, 6 * d:6 * d + N_HEADS]).astype(BF16)
        proj, f = _inproj(h, norm1_g[l][None], w_all, w_f)

        bf = _pad_lanes(b_forget[l][None])
        gq2 = jnp.tile(q_norm_g[l], 2)[None] * (scale * LOG2E)
        gk2 = jnp.tile(k_norm_g[l], 2)[None]
        qa, ka, va = _prep(proj, f, bf, gq2, gk2, batch, lp)
        attn = _attention(qa, ka, va)

        w_r = _pad_lanes(jnp.concatenate([w_router_group[l], w_router_expert[l]], axis=1))
        wr_hi = w_r.astype(BF16)
        wr2 = jnp.concatenate([wr_hi, (w_r - wr_hi.astype(F32)).astype(BF16)], axis=1)
        b_r = _pad_lanes(jnp.concatenate([b_router_group[l], b_router_expert[l]])[None])
        cw = jnp.pad(conv_w[l], ((0, 8 - conv_w.shape[1]), (0, 0)))
        h_mid, xp, route = _merge(
            proj, attn, h, cw, w_conv_out[l].astype(BF16), w_att_out[l].astype(BF16),
            w_merge_out[l].astype(BF16), norm2_g[l][None], wr2, b_r, lp)

        pos, blk = _positions(route, nbp)
        dest = pos[0:2].reshape(-1)
        xs = _dispatch(dest, xp, p_rows)
        ys = _experts(blk[:n_blocks, 0], blk[:n_blocks, 1], xs,
                      w_exp_gate, w_exp_up, w_exp_down, l)
        g = _gather_rows(ys, dest)
        if l + 1 < depth:
            h = _combine(h_mid, route, g)
        else:
            out = _combine_output(h_mid, route, g, batch, lp, seq)

    return out.reshape(batch, seq, d)
```
